```python
import jax, jax.numpy as jnp
from jax import lax
import numpy as np

D_MODEL = 2048
BATCH = 16
SEQ = 256
DEPTH = 2
DEC_BATCH = 2
DEC_SEQ = 2048
PAST_LEN = 512

GRID_W = 64
N_EVEN = (DEPTH + 1) // 2
N_ODD = DEPTH // 2
A_HEADS = 8
A_DK = 128
A_DV = 128
A_KDIM = A_HEADS * A_DK
A_VDIM = A_HEADS * A_DV
B_HEADS = 4
B_DK = 128
B_DV = 256
B_KDIM = B_HEADS * B_DK
B_VDIM = B_HEADS * B_DV
GLA_RANK = 16
GLA_GATE_NORM = 16.0
CHUNK = 32
EVEN_SPLITS = (A_KDIM, A_KDIM, A_KDIM, A_VDIM, A_VDIM, B_KDIM, B_KDIM, B_VDIM, B_VDIM, GLA_RANK, GLA_RANK)
EVEN_IN = sum(EVEN_SPLITS)
LRU_WIDTH = D_MODEL
LRU_BLOCKS = 8
LRU_BW = LRU_WIDTH // LRU_BLOCKS
LRU_C = 8.0
CONV_W = 4
CONV_PAD_L = 2
D_FF = (8 * D_MODEL + 3 * 256 - 1) // (3 * 256) * 256
EPS = 1e-6

kernel_name = 'hybrid_hgrn2_gla_rglru_prefix_diffusion_step'


def _offsets(sizes):
    return [int(v) for v in np.cumsum(sizes)[:-1]]


def _rms_norm(x, g):
    xf = x.astype(jnp.float32)
    y = xf * lax.rsqrt(jnp.mean(xf * xf, axis=-1, keepdims=True) + EPS)
    return (y * g.astype(jnp.float32)).astype(x.dtype)


def _modulation(cvec, w, b):
    m = jax.nn.silu(cvec) @ w + b
    return [u[:, None, :] for u in jnp.split(m, 6, axis=-1)]


def _chunk_gla(q, k, v, log_a, s0):
    bsz, t, h, _ = q.shape
    dv = v.shape[-1]
    n = t // CHUNK
    f32 = jnp.float32

    def to_chunks(z):
        return jnp.moveaxis(z.astype(f32).reshape(bsz, n, CHUNK, h, z.shape[-1]), 1, 0)

    qc, kc, vc, gc = to_chunks(q), to_chunks(k), to_chunks(v), to_chunks(log_a)
    causal = jnp.tril(jnp.ones((CHUNK, CHUNK), dtype=bool))

    def step(s, inp):
        qi, ki, vi, gi = inp
        b = jnp.cumsum(gi, axis=1)
        q_dec = qi * jnp.exp(b)
        k_inv = ki * jnp.exp(-b)
        att = jnp.where(causal, jnp.einsum('bihd,bjhd->bhij', q_dec, k_inv), 0.0)
        o = jnp.einsum('bihd,bhdv->bihv', q_dec, s) + jnp.einsum('bhij,bjhv->bihv', att, vi)
        b_last = b[:, -1]
        k_tail = ki * jnp.exp(b_last[:, None] - b)
        s = jnp.exp(b_last)[..., None] * s + jnp.einsum('bjhd,bjhv->bhdv', k_tail, vi)
        return s, o

    s_fin, o = lax.scan(step, s0.astype(f32), (qc, kc, vc, gc))
    return jnp.moveaxis(o, 0, 1).reshape(bsz, t, h, dv), s_fin


def _bidir_gla(q, k_f, k_b, v, g_f, g_b, s0):
    rev = lambda z: jnp.flip(z, axis=1)
    o_f, s_f = _chunk_gla(q, k_f, v, g_f, s0[:, 0])
    o_b, s_b = _chunk_gla(rev(q), rev(k_b), rev(v), rev(g_b), s0[:, 1])
    return o_f + rev(o_b), jnp.stack([s_f, s_b], axis=1)


def _gated_head_norm(o, gate, gain):
    bsz, t, h, dv = o.shape
    o = o * lax.rsqrt(jnp.mean(o * o, axis=-1, keepdims=True) + EPS) * gain.astype(jnp.float32)
    return o.reshape(bsz, t, h * dv) * jax.nn.silu(gate.astype(jnp.float32))


def _even_mixer(h, s_a0, s_b0, lb, w_in, w_alpha, b_alpha, a_gain, b_gain, w_out):
    bsz, t, _ = h.shape
    f32 = jnp.float32
    aq, af_f, af_b, ai, ag, bq, bk, bv, bg, blr_f, blr_b = jnp.split(h @ w_in, _offsets(EVEN_SPLITS), axis=-1)
    hd = lambda u, n: u.astype(f32).reshape(bsz, t, n, -1)
    lbf = lb.astype(f32)
    f_f = lbf + (1.0 - lbf) * jax.nn.sigmoid(af_f.astype(f32))
    f_b = lbf + (1.0 - lbf) * jax.nn.sigmoid(af_b.astype(f32))
    qa = hd(aq, A_HEADS) * (A_DK ** -0.5)
    o_a, s_a = _bidir_gla(qa, hd(1.0 - f_f, A_HEADS), hd(1.0 - f_b, A_HEADS), hd(ai, A_HEADS),
                          hd(jnp.log(f_f), A_HEADS), hd(jnp.log(f_b), A_HEADS), s_a0)
    g_f = jax.nn.log_sigmoid((blr_f @ w_alpha[0] + b_alpha[0]).astype(f32)) / GLA_GATE_NORM
    g_b = jax.nn.log_sigmoid((blr_b @ w_alpha[1] + b_alpha[1]).astype(f32)) / GLA_GATE_NORM
    qb = hd(bq, B_HEADS) * (B_DK ** -0.5)
    kb = hd(bk, B_HEADS)
    o_b, s_b = _bidir_gla(qb, kb, kb, hd(bv, B_HEADS), hd(g_f, B_HEADS), hd(g_b, B_HEADS), s_b0)
    merged = jnp.concatenate([_gated_head_norm(o_a, ag, a_gain), _gated_head_norm(o_b, bg, b_gain)], axis=-1)
    return merged.astype(h.dtype) @ w_out, s_a, s_b


def _centred_dwconv(u, w, b):
    n = u.shape[1]
    up = jnp.pad(u, ((0, 0), (CONV_PAD_L, CONV_W - 1 - CONV_PAD_L), (0, 0)))
    out = b
    for j in range(CONV_W):
        out = out + up[:, j:j + n] * w[j]
    return out


def _rglru(xc, w_rg, b_rg, w_ig, b_ig, lam, h0):
    bsz, t, _ = xc.shape
    f32 = jnp.float32
    xb = xc.reshape(bsz, t, LRU_BLOCKS, LRU_BW)
    r = jax.nn.sigmoid(jnp.einsum('btnc,ncd->btnd', xb, w_rg.astype(f32)).reshape(bsz, t, -1) + b_rg.astype(f32))
    i = jax.nn.sigmoid(jnp.einsum('btnc,ncd->btnd', xb, w_ig.astype(f32)).reshape(bsz, t, -1) + b_ig.astype(f32))
    log_a = -LRU_C * r * jax.nn.softplus(-lam.astype(f32))
    a = jnp.exp(log_a)
    u = jnp.sqrt(-jnp.expm1(2.0 * log_a)) * (i * xc)

    def combine(left, right):
        a1, b1 = left
        a2, b2 = right
        return a1 * a2, a2 * b1 + b2

    a_cum, hs = lax.associative_scan(combine, (a, u), axis=1)
    hs = hs + a_cum * h0.astype(f32)[:, None, :]
    return hs, hs[:, -1]


def _odd_mixer(h, s0, latent, w_in, conv_w, conv_b, rg_w, rg_b, ig_w, ig_b, lam, w_out):
    bsz, t, _ = h.shape
    f32 = jnp.float32
    xr, gate = jnp.split(h @ w_in, 2, axis=-1)
    if latent:
        rows = t // GRID_W
        xc = _centred_dwconv(xr.reshape(bsz * rows, GRID_W, LRU_WIDTH), conv_w, conv_b).reshape(bsz, t, LRU_WIDTH)
    else:
        xc = _centred_dwconv(xr, conv_w, conv_b)
    xc = xc.astype(f32)
    y_f, s_f = _rglru(xc, rg_w[0], rg_b[0], ig_w[0], ig_b[0], lam[0], s0[:, 0])
    y_b, s_b = _rglru(jnp.flip(xc, axis=1), rg_w[1], rg_b[1], ig_w[1], ig_b[1], lam[1], s0[:, 1])
    y = (y_f + jnp.flip(y_b, axis=1)) * jax.nn.gelu(gate.astype(f32), approximate=True)
    return y.astype(h.dtype) @ w_out, jnp.stack([s_f, s_b], axis=1)


def _swiglu(h, w_in, w_out):
    gt, up = jnp.split(h @ w_in, 2, axis=-1)
    return (jax.nn.silu(gt) * up) @ w_out


def _trunk(x, cvec, s_hgrn, s_gla, s_lru, latent, p):
    lb_all = jnp.cumsum(jax.nn.softmax(p['hgrn_lower_bounds'].astype(jnp.float32), axis=0), axis=0)
    out_h, out_g, out_r = [], [], []
    for l in range(DEPTH):
        sh1, sc1, g1, sh2, sc2, g2 = _modulation(cvec, p['w_mod'][l], p['b_mod'][l])
        h = _rms_norm(x, p['norm1_g'][l]) * (1 + sc1) + sh1
        e = l // 2
        if l % 2 == 0:
            mix, sa, sb = _even_mixer(h, s_hgrn[:, e], s_gla[:, e], lb_all[l], p['even_w_in'][e],
                                      p['gla_w_alpha'][e], p['gla_b_alpha'][e], p['hgrn_norm_g'][e],
                                      p['gla_norm_g'][e], p['even_w_out'][e])
            out_h.append(sa)
            out_g.append(sb)
        else:
            mix, sr = _odd_mixer(h, s_lru[:, e], latent, p['odd_w_in'][e], p['conv_w'][e], p['conv_b'][e],
                                 p['rg_w'][e], p['rg_b'][e], p['ig_w'][e], p['ig_b'][e],
                                 p['lru_lambda'][e], p['odd_w_out'][e])
            out_r.append(sr)
        x = x + g1 * mix
        h = _rms_norm(x, p['norm2_g'][l]) * (1 + sc2) + sh2
        x = x + g2 * _swiglu(h, p['ffn_w_in'][l], p['ffn_w_out'][l])
    y = _rms_norm(x, p['final_norm_g'])
    return y, jnp.stack(out_h, axis=1), jnp.stack(out_g, axis=1), jnp.stack(out_r, axis=1)


def setup_inputs(seed: int = 0) -> dict:
    key = jax.random.key(seed)
    ks = iter(jax.random.split(key, 40))
    f32 = jnp.float32
    nrm = lambda shape, scale: scale * jax.random.normal(next(ks), shape, f32)
    D = D_MODEL
    x_prompt = nrm((BATCH, SEQ, D), 1.0)
    x_sample = nrm((DEC_BATCH, DEC_SEQ, D), 1.0)
    state_hgrn = nrm((DEC_BATCH, N_EVEN, 2, A_HEADS, A_DK, A_DV), 0.5)
    state_gla = nrm((DEC_BATCH, N_EVEN, 2, B_HEADS, B_DK, B_DV), 1.0)
    state_rglru = nrm((DEC_BATCH, N_ODD, 2, LRU_WIDTH), 0.5)
    c = nrm((DEC_BATCH, D), 1.0)
    c_ctx = nrm((D,), 1.0)
    norm1_g = 1.0 + nrm((DEPTH, D), 0.05)
    norm2_g = 1.0 + nrm((DEPTH, D), 0.05)
    w_mod = nrm((DEPTH, D, 6 * D), D ** -0.5)
    b_mod = nrm((DEPTH, 6 * D), 0.01)
    ffn_w_in = nrm((DEPTH, D, 2 * D_FF), D ** -0.5)
    ffn_w_out = nrm((DEPTH, D_FF, D), D_FF ** -0.5)
    hgrn_lower_bounds = nrm((DEPTH + 1, A_KDIM), 0.1)
    even_w_in = nrm((N_EVEN, D, EVEN_IN), D ** -0.5)
    gla_w_alpha = nrm((N_EVEN, 2, GLA_RANK, B_KDIM), GLA_RANK ** -0.5)
    gla_b_alpha = nrm((N_EVEN, 2, B_KDIM), 0.1)
    hgrn_norm_g = 1.0 + nrm((N_EVEN, A_DV), 0.05)
    gla_norm_g = 1.0 + nrm((N_EVEN, B_DV), 0.05)
    even_w_out = nrm((N_EVEN, A_VDIM + B_VDIM, D), (A_VDIM + B_VDIM) ** -0.5)
    odd_w_in = nrm((N_ODD, D, 2 * LRU_WIDTH), D ** -0.5)
    conv_w = nrm((N_ODD, CONV_W, LRU_WIDTH), CONV_W ** -0.5)
    conv_b = nrm((N_ODD, LRU_WIDTH), 0.01)
    rg_w = nrm((N_ODD, 2, LRU_BLOCKS, LRU_BW, LRU_BW), LRU_BW ** -0.5)
    rg_b = nrm((N_ODD, 2, LRU_WIDTH), 0.01)
    ig_w = nrm((N_ODD, 2, LRU_BLOCKS, LRU_BW, LRU_BW), LRU_BW ** -0.5)
    ig_b = nrm((N_ODD, 2, LRU_WIDTH), 0.01)
    a_c = jax.random.uniform(next(ks), (N_ODD, 2, LRU_WIDTH), f32, 0.9, 0.999)
    a_base = a_c ** (1.0 / LRU_C)
    lru_lambda = jnp.log(a_base) - jnp.log1p(-a_base)
    odd_w_out = nrm((N_ODD, LRU_WIDTH, D), LRU_WIDTH ** -0.5)
    final_norm_g = 1.0 + nrm((D,), 0.05)
    return {'x_prompt': x_prompt, 'x_sample': x_sample, 'state_hgrn': state_hgrn, 'state_gla': state_gla,
            'state_rglru': state_rglru, 'c': c, 'c_ctx': c_ctx, 'norm1_g': norm1_g, 'norm2_g': norm2_g,
            'w_mod': w_mod, 'b_mod': b_mod, 'ffn_w_in': ffn_w_in, 'ffn_w_out': ffn_w_out,
            'hgrn_lower_bounds': hgrn_lower_bounds, 'even_w_in': even_w_in, 'gla_w_alpha': gla_w_alpha,
            'gla_b_alpha': gla_b_alpha, 'hgrn_norm_g': hgrn_norm_g, 'gla_norm_g': gla_norm_g,
            'even_w_out': even_w_out, 'odd_w_in': odd_w_in, 'conv_w': conv_w, 'conv_b': conv_b,
            'rg_w': rg_w, 'rg_b': rg_b, 'ig_w': ig_w, 'ig_b': ig_b, 'lru_lambda': lru_lambda,
            'odd_w_out': odd_w_out, 'final_norm_g': final_norm_g}


def reference(x_prompt, x_sample, state_hgrn, state_gla, state_rglru, c, c_ctx, norm1_g, norm2_g, w_mod, b_mod,
              ffn_w_in, ffn_w_out, hgrn_lower_bounds, even_w_in, gla_w_alpha, gla_b_alpha, hgrn_norm_g,
              gla_norm_g, even_w_out, odd_w_in, conv_w, conv_b, rg_w, rg_b, ig_w, ig_b, lru_lambda, odd_w_out,
              final_norm_g):
    p = dict(norm1_g=norm1_g, norm2_g=norm2_g, w_mod=w_mod, b_mod=b_mod, ffn_w_in=ffn_w_in,
             ffn_w_out=ffn_w_out, hgrn_lower_bounds=hgrn_lower_bounds, even_w_in=even_w_in,
             gla_w_alpha=gla_w_alpha, gla_b_alpha=gla_b_alpha, hgrn_norm_g=hgrn_norm_g, gla_norm_g=gla_norm_g,
             even_w_out=even_w_out, odd_w_in=odd_w_in, conv_w=conv_w, conv_b=conv_b, rg_w=rg_w, rg_b=rg_b,
             ig_w=ig_w, ig_b=ig_b, lru_lambda=lru_lambda, odd_w_out=odd_w_out, final_norm_g=final_norm_g)
    bp = x_prompt.shape[0]
    z_hgrn = jnp.zeros((bp, N_EVEN, 2, A_HEADS, A_DK, A_DV), jnp.float32)
    z_gla = jnp.zeros((bp, N_EVEN, 2, B_HEADS, B_DK, B_DV), jnp.float32)
    z_lru = jnp.zeros((bp, N_ODD, 2, LRU_WIDTH), jnp.float32)
    y_prompt, new_hgrn, new_gla, new_lru = _trunk(x_prompt, c_ctx[None, :], z_hgrn, z_gla, z_lru, False, p)
    y_sample = _trunk(x_sample, c, state_hgrn, state_gla, state_rglru, True, p)[0]
    dt = x_prompt.dtype
    return (y_prompt, y_sample, new_hgrn.astype(dt), new_gla.astype(dt), new_lru.astype(dt))
```

```python
import functools

import jax
import jax.numpy as jnp
from jax import lax
from jax.experimental import pallas as pl
from jax.experimental.pallas import tpu as pltpu

F32 = jnp.float32
BF16 = jnp.bfloat16

D_MODEL = 2048
BATCH = 16
SEQ = 256
DEPTH = 2
DEC_BATCH = 2
DEC_SEQ = 2048
GRID_W = 64
N_EVEN = (DEPTH + 1) // 2
N_ODD = DEPTH // 2
A_HEADS = 8
A_DK = 128
A_DV = 128
A_KDIM = A_HEADS * A_DK
A_VDIM = A_HEADS * A_DV
B_HEADS = 4
B_DK = 128
B_DV = 256
B_KDIM = B_HEADS * B_DK
B_VDIM = B_HEADS * B_DV
GLA_RANK = 16
GLA_GATE_NORM = 16.0
CHUNK = 32
EVEN_MAIN = 3 * A_KDIM + 2 * A_VDIM + 2 * B_KDIM + 2 * B_VDIM
LRU_WIDTH = D_MODEL
LRU_BLOCKS = 8
LRU_BW = LRU_WIDTH // LRU_BLOCKS
LRU_C = 8.0
CONV_W = 4
D_FF = (8 * D_MODEL + 3 * 256 - 1) // (3 * 256) * 256
EPS = 1e-6

N_PROMPT = BATCH * SEQ
N_SAMPLE = DEC_BATCH * DEC_SEQ
TOKENS = N_PROMPT + N_SAMPLE
MOD_ROWS = 8
SUBLANES = 8
LANES = 128
TIME_TILE = 256
VMEM_LIMIT = 56 * 1024 * 1024


def _sigmoid(x):
    return 1.0 / (1.0 + jnp.exp(-x))


def _silu(x):
    return x * _sigmoid(x)


def _softplus(x):
    return jnp.maximum(x, 0.0) + jnp.log1p(jnp.exp(-jnp.abs(x)))


def _gelu_tanh(x):
    c = 0.7978845608028654
    return x * (0.5 * (1.0 + jnp.tanh(c * (x + 0.044715 * (x * x * x)))))


def _mod_row(tile, tile_rows):
    r0 = tile * tile_rows
    return jnp.where(r0 < N_PROMPT, 0, 1 + (r0 - N_PROMPT) // DEC_SEQ)


def _mod_spec(layer, which, tile_rows, width=D_MODEL):
    base = (layer * 6 + which) * MOD_ROWS
    if width == D_MODEL:
        return pl.BlockSpec((1, 1, width), lambda i, j: (base + _mod_row(i, tile_rows), 0, 0))
    return pl.BlockSpec((1, 1, width), lambda i, j: (base + _mod_row(i, tile_rows), 0, j))


def _params(*sem):
    return pltpu.CompilerParams(dimension_semantics=sem, vmem_limit_bytes=VMEM_LIMIT)


def _mod_kernel(c_ref, w_ref, b_ref, o_ref):
    s = _silu(c_ref[...]).astype(BF16)
    o_ref[0, 0] = jnp.dot(s, w_ref[0].astype(BF16), preferred_element_type=F32) + b_ref[0]


def _modulation(c3, w_mod, b_mod):
    tn = 1024
    nb = D_MODEL // tn
    out = pl.pallas_call(
        _mod_kernel,
        grid=(DEPTH, 6, nb),
        in_specs=[pl.BlockSpec((MOD_ROWS, D_MODEL), lambda l, k, j: (0, 0)),
                  pl.BlockSpec((1, D_MODEL, tn), lambda l, k, j: (l, 0, k * nb + j)),
                  pl.BlockSpec((1, 1, tn), lambda l, k, j: (l, 0, k * nb + j))],
        out_specs=pl.BlockSpec((1, 1, MOD_ROWS, tn), lambda l, k, j: (l, k, 0, j)),
        out_shape=jax.ShapeDtypeStruct((DEPTH, 6, MOD_ROWS, D_MODEL), F32),
        compiler_params=_params("arbitrary", "arbitrary", "arbitrary"),
        name="modulation",
    )(c3, w_mod, b_mod.reshape(DEPTH, 1, 6 * D_MODEL))
    return out.reshape(DEPTH * 6 * MOD_ROWS, 1, D_MODEL)


def _norm_mod(x, g, sc, sh):
    ms = jnp.mean(x * x, axis=-1, keepdims=True)
    y = (x * lax.rsqrt(ms + EPS)) * g
    return (y * (1.0 + sc) + sh).astype(BF16)


def _inproj_kernel(x_ref, g_ref, sc_ref, sh_ref, w_ref, o_ref, h_scr):
    @pl.when(pl.program_id(1) == 0)
    def _():
        h_scr[...] = _norm_mod(x_ref[...], g_ref[...], sc_ref[0], sh_ref[0])

    o_ref[...] = jnp.dot(h_scr[...], w_ref[...].astype(BF16), preferred_element_type=F32)


def _inproj_lowrank_kernel(x_ref, g_ref, sc_ref, sh_ref, w_ref, wl_ref, o_ref, ol_ref, h_scr):
    @pl.when(pl.program_id(1) == 0)
    def _():
        h = _norm_mod(x_ref[...], g_ref[...], sc_ref[0], sh_ref[0])
        h_scr[...] = h
        ol_ref[...] = jnp.dot(h, wl_ref[...].astype(BF16), preferred_element_type=F32)

    o_ref[...] = jnp.dot(h_scr[...], w_ref[...].astype(BF16), preferred_element_type=F32)


def _inproj(x, g, mod, layer, w, widx, n_main, w_low=None):
    tm, tn = 1024, 512
    grid = (TOKENS // tm, n_main // tn)
    in_specs = [pl.BlockSpec((tm, D_MODEL), lambda i, j: (i, 0)),
                pl.BlockSpec((1, D_MODEL), lambda i, j: (0, 0)),
                _mod_spec(layer, 1, tm),
                _mod_spec(layer, 0, tm),
                pl.BlockSpec((None, D_MODEL, tn), lambda i, j: (widx, 0, j))]
    out_specs = pl.BlockSpec((tm, tn), lambda i, j: (i, j))
    out_shape = jax.ShapeDtypeStruct((TOKENS, n_main), F32)
    args = [x, g, mod, mod, w]
    body = _inproj_kernel
    if w_low is not None:
        nl = w_low.shape[1]
        in_specs.append(pl.BlockSpec((D_MODEL, nl), lambda i, j: (0, 0)))
        out_specs = [out_specs, pl.BlockSpec((tm, nl), lambda i, j: (i, 0))]
        out_shape = [out_shape, jax.ShapeDtypeStruct((TOKENS, nl), F32)]
        args.append(w_low)
        body = _inproj_lowrank_kernel
    return pl.pallas_call(
        body, grid=grid, in_specs=in_specs, out_specs=out_specs, out_shape=out_shape,
        scratch_shapes=[pltpu.VMEM((tm, D_MODEL), BF16)],
        compiler_params=_params("parallel", "arbitrary"),
        name="inproj",
    )(*args)


def _outproj_kernel(a_ref, w_ref, x_ref, gate_ref, o_ref):
    acc = jnp.dot(a_ref[...], w_ref[...].astype(BF16), preferred_element_type=F32)
    o_ref[...] = x_ref[...] + gate_ref[0] * acc


def _outproj(a, w, widx, x, mod, layer, which, tm, tn):
    k = a.shape[1]
    return pl.pallas_call(
        _outproj_kernel, grid=(TOKENS // tm, D_MODEL // tn),
        in_specs=[pl.BlockSpec((tm, k), lambda i, j: (i, 0)),
                  pl.BlockSpec((None, k, tn), lambda i, j: (widx, 0, j)),
                  pl.BlockSpec((tm, tn), lambda i, j: (i, j)),
                  _mod_spec(layer, which, tm, tn)],
        out_specs=pl.BlockSpec((tm, tn), lambda i, j: (i, j)),
        out_shape=jax.ShapeDtypeStruct((TOKENS, D_MODEL), F32),
        input_output_aliases={2: 0},
        compiler_params=_params("parallel", "arbitrary"),
        name="outproj",
    )(a, w, x, mod)


def _ffn_up_kernel(x_ref, g_ref, sc_ref, sh_ref, wg_ref, wu_ref, o_ref, h_scr):
    @pl.when(pl.program_id(1) == 0)
    def _():
        h_scr[...] = _norm_mod(x_ref[...], g_ref[...], sc_ref[0], sh_ref[0])

    h = h_scr[...]
    gt = jnp.dot(h, wg_ref[...].astype(BF16), preferred_element_type=F32)
    up = jnp.dot(h, wu_ref[...].astype(BF16), preferred_element_type=F32)
    o_ref[...] = (_silu(gt) * up).astype(BF16)


def _ffn_up(x, g, mod, layer, w_in):
    tm, tf = 1024, 512
    nf = D_FF // tf
    return pl.pallas_call(
        _ffn_up_kernel, grid=(TOKENS // tm, nf),
        in_specs=[pl.BlockSpec((tm, D_MODEL), lambda i, j: (i, 0)),
                  pl.BlockSpec((1, D_MODEL), lambda i, j: (0, 0)),
                  _mod_spec(layer, 4, tm),
                  _mod_spec(layer, 3, tm),
                  pl.BlockSpec((None, D_MODEL, tf), lambda i, j: (layer, 0, j)),
                  pl.BlockSpec((None, D_MODEL, tf), lambda i, j: (layer, 0, nf + j))],
        out_specs=pl.BlockSpec((tm, tf), lambda i, j: (i, j)),
        out_shape=jax.ShapeDtypeStruct((TOKENS, D_FF), BF16),
        scratch_shapes=[pltpu.VMEM((tm, D_MODEL), BF16)],
        compiler_params=_params("parallel", "arbitrary"),
        name="ffn_up",
    )(x, g, mod, mod, w_in, w_in)


def _final_norm_kernel(x_ref, g_ref, o_ref):
    x = x_ref[...]
    ms = jnp.mean(x * x, axis=-1, keepdims=True)
    o_ref[...] = (x * lax.rsqrt(ms + EPS)) * g_ref[...]


def _final_norm(x, g, row0, rows):
    tm = 512
    t0 = row0 // tm
    return pl.pallas_call(
        _final_norm_kernel, grid=(rows // tm,),
        in_specs=[pl.BlockSpec((tm, D_MODEL), lambda i: (t0 + i, 0)),
                  pl.BlockSpec((1, D_MODEL), lambda i: (0, 0))],
        out_specs=pl.BlockSpec((tm, D_MODEL), lambda i: (i, 0)),
        out_shape=jax.ShapeDtypeStruct((rows, D_MODEL), F32),
        compiler_params=_params("parallel"),
        name="final_norm",
    )(x, g)


def _chunk_cumsum(g, pos, reverse):
    x = g
    s = 1
    while s < CHUNK:
        if reverse:
            x = x + jnp.where(pos < CHUNK - s, pltpu.roll(x, CHUNK - s, 0), 0.0)
        else:
            x = x + jnp.where(pos >= s, pltpu.roll(x, s, 0), 0.0)
        s *= 2
    return x


def _gla_kernel(*refs, variant, layer, seq_len, dv, has_s0):
    refs = list(refs)
    if variant == "hgrn":
        q_ref, ff_ref, fb_ref, v_ref, gate_ref, lb_ref, gain_ref = refs[:7]
        refs = refs[7:]
        gate_in = (ff_ref, fb_ref)
    else:
        q_ref, k_ref, v_ref, gate_ref, lr_ref, wa_ref, ba_ref, gain_ref = refs[:8]
        refs = refs[8:]
    s0_ref = refs.pop(0) if has_s0 else None
    refs.pop(0)
    out_ref, st_ref = refs[:2]
    o_scr = refs[2]
    g_scr = refs[3:]

    t = seq_len
    n = t // CHUNK
    pos = lax.broadcasted_iota(jnp.int32, (CHUNK, A_DK), 0)
    ri = lax.broadcasted_iota(jnp.int32, (CHUNK, CHUNK), 0)
    ci = lax.broadcasted_iota(jnp.int32, (CHUNK, CHUNK), 1)
    masks = (ci <= ri, ci >= ri)
    nt_dims = (((1,), (1,)), ((), ()))
    tn_dims = (((0,), (0,)), ((), ()))
    gain = gain_ref[...]

    if variant == "hgrn":
        lbr = lb_ref[...]
        ex = jnp.exp(lbr - jnp.max(lbr, axis=0, keepdims=True))
        lb = jnp.sum(ex[:layer + 1], axis=0, keepdims=True) / jnp.sum(ex, axis=0, keepdims=True)
    else:
        def decay_tile(i, carry):
            rows = pl.ds(pl.multiple_of(i * TIME_TILE, TIME_TILE), TIME_TILE)
            lr = lr_ref[rows, :].astype(BF16)
            for d in range(2):
                z = jnp.dot(lr, wa_ref[d].astype(BF16), preferred_element_type=F32) + ba_ref[d:d + 1, :]
                g_scr[d][rows, :] = -_softplus(-z) / GLA_GATE_NORM
            return carry

        lax.fori_loop(0, t // TIME_TILE, decay_tile, 0)

    def chunk_step(c, s_t, d):
        rows = pl.ds(pl.multiple_of(c * CHUNK, CHUNK), CHUNK)
        q = q_ref[rows, :] * (A_DK ** -0.5)
        if variant == "hgrn":
            f = lb + (1.0 - lb) * _sigmoid(gate_in[d][rows, :])
            k = 1.0 - f
            g = jnp.log(f)
        else:
            k = k_ref[rows, :]
            g = g_scr[d][rows, :]
        b = _chunk_cumsum(g, pos, d == 1)
        tot = b[0:1, :] if d == 1 else b[CHUNK - 1:CHUNK, :]
        qd = (q * jnp.exp(b)).astype(BF16)
        ki = (k * jnp.exp(-b)).astype(BF16)
        kt = (k * jnp.exp(tot - b)).astype(BF16)
        vb = v_ref[rows, :].astype(BF16)
        att = lax.dot_general(qd, ki, nt_dims, preferred_element_type=F32)
        att = jnp.where(masks[d], att, 0.0).astype(BF16)
        o = (lax.dot_general(qd, s_t.astype(BF16), nt_dims, preferred_element_type=F32)
             + jnp.dot(att, vb, preferred_element_type=F32))
        s_new = jnp.exp(tot) * s_t + lax.dot_general(vb, kt, tn_dims, preferred_element_type=F32)
        return rows, o, s_new

    def first_half(i, carry):
        s_f, s_b = carry
        rows_f, o_f, s_f = chunk_step(i, s_f, 0)
        o_scr[rows_f, :] = o_f
        rows_b, o_b, s_b = chunk_step(n - 1 - i, s_b, 1)
        o_scr[rows_b, :] = o_b
        return s_f, s_b

    def finish(rows, o):
        o = o + o_scr[rows, :]
        ms = jnp.mean(o * o, axis=-1, keepdims=True)
        o = (o * lax.rsqrt(ms + EPS)) * gain
        out_ref[rows, :] = (o * _silu(gate_ref[rows, :])).astype(out_ref.dtype)

    def second_half(i, carry):
        s_f, s_b = carry
        rows_f, o_f, s_f = chunk_step(i, s_f, 0)
        finish(rows_f, o_f)
        rows_b, o_b, s_b = chunk_step(n - 1 - i, s_b, 1)
        finish(rows_b, o_b)
        return s_f, s_b

    if has_s0:
        init = (s0_ref[0, 0, 0, 0].T, s0_ref[0, 0, 1, 0].T)
    else:
        init = (jnp.zeros((dv, A_DK), F32), jnp.zeros((dv, A_DK), F32))
    carry = lax.fori_loop(0, n // 2, first_half, init)
    s_f, s_b = lax.fori_loop(n // 2, n, second_half, carry)
    st_ref[0, 0, 0, 0] = s_f.T
    st_ref[0, 0, 1, 0] = s_b.T


def _gla_scan(variant, layer, e, proj, lowrank, prev, s0, n_seq, seq_len, row0, small):
    t = seq_len
    tb0 = row0 // t
    heads, dv = (A_HEADS, A_DV) if variant == "hgrn" else (B_HEADS, B_DV)

    def col_spec(width, col0):
        blk0 = col0 // width
        return pl.BlockSpec((t, width), lambda s, h: (tb0 + s, blk0 + h))

    if variant == "hgrn":
        o_q, o_ff, o_fb, o_v, o_g = 0, A_KDIM, 2 * A_KDIM, 3 * A_KDIM, 3 * A_KDIM + A_VDIM
        in_specs = [col_spec(A_DK, o_q), col_spec(A_DK, o_ff), col_spec(A_DK, o_fb),
                    col_spec(dv, o_v), col_spec(dv, o_g),
                    pl.BlockSpec((DEPTH + 1, A_DK), lambda s, h: (0, h)),
                    pl.BlockSpec((1, dv), lambda s, h: (0, 0))]
        args = [proj] * 5 + [small["lb"], small["gain"]]
        out_col0 = 0
        scratch = [pltpu.VMEM((t, dv), F32)]
    else:
        base = 3 * A_KDIM + 2 * A_VDIM
        o_q, o_k, o_v, o_g = base, base + B_KDIM, base + 2 * B_KDIM, base + 2 * B_KDIM + B_VDIM
        in_specs = [col_spec(B_DK, o_q), col_spec(B_DK, o_k),
                    col_spec(dv, o_v), col_spec(dv, o_g),
                    pl.BlockSpec((t, 2 * GLA_RANK), lambda s, h: (tb0 + s, 0)),
                    pl.BlockSpec((2, 2 * GLA_RANK, B_DK), lambda s, h: (0, 0, h)),
                    pl.BlockSpec((2, B_DK), lambda s, h: (0, h)),
                    pl.BlockSpec((1, dv), lambda s, h: (0, 0))]
        args = [proj] * 4 + [lowrank, small["wa"], small["ba"], small["gain"]]
        out_col0 = A_VDIM
        scratch = [pltpu.VMEM((t, dv), F32), pltpu.VMEM((t, B_DK), F32), pltpu.VMEM((t, B_DK), F32)]
    has_s0 = s0 is not None
    if has_s0:
        in_specs.append(pl.BlockSpec((1, 1, 2, 1, A_DK, dv), lambda s, h: (s, e, 0, h, 0, 0)))
        args.append(s0)
    aliases = {}
    in_specs.append(pl.BlockSpec(memory_space=pl.ANY))
    if prev is None:
        args.append(jnp.zeros((SUBLANES, LANES), BF16))
    else:
        aliases = {len(args): 0}
        args.append(prev)
    out_blk0 = out_col0 // dv
    merged, states = pl.pallas_call(
        functools.partial(_gla_kernel, variant=variant, layer=layer, seq_len=t, dv=dv, has_s0=has_s0),
        grid=(n_seq, heads),
        in_specs=in_specs,
        out_specs=[pl.BlockSpec((t, dv), lambda s, h: (tb0 + s, out_blk0 + h)),
                   pl.BlockSpec((1, 1, 2, 1, A_DK, dv), lambda s, h: (s, 0, 0, h, 0, 0))],
        out_shape=[jax.ShapeDtypeStruct((TOKENS, A_VDIM + B_VDIM), BF16),
                   jax.ShapeDtypeStruct((n_seq, 1, 2, heads, A_DK, dv), F32)],
        scratch_shapes=scratch,
        input_output_aliases=aliases,
        compiler_params=_params("parallel", "arbitrary"),
        name=f"{variant}_scan",
    )(*args)
    return merged, states


def _lru_kernel(*refs, seq_len, row_len, has_s0):
    refs = list(refs)
    xr_ref, gate_ref, cw_ref, cb_ref, rgw_ref, rgb_ref, igw_ref, igb_ref, lam_ref = refs[:9]
    refs = refs[9:]
    s0_ref = refs.pop(0) if has_s0 else None
    refs.pop(0)
    out_ref, st_ref = refs[:2]
    xc_scr, y_scr, a_scr, u_scr, h_scr = refs[2:]

    t = seq_len
    w = LRU_BW
    tl = TIME_TILE
    n_tiles = t // tl
    n_groups = tl // SUBLANES
    pos = lax.broadcasted_iota(jnp.int32, (tl, w), 0) % row_len
    sub = lax.broadcasted_iota(jnp.int32, (tl, w), 0) % SUBLANES
    cw = cw_ref[...]

    def conv(x):
        xc = cb_ref[...] + jnp.where(pos >= 2, pltpu.roll(x, 2, 0), 0.0) * cw[0:1, :]
        xc = xc + jnp.where(pos >= 1, pltpu.roll(x, 1, 0), 0.0) * cw[1:2, :]
        xc = xc + x * cw[2:3, :]
        return xc + jnp.where(pos < row_len - 1, pltpu.roll(x, tl - 1, 0), 0.0) * cw[3:4, :]

    def tile_pass(d):
        sp = _softplus(-lam_ref[d:d + 1, :])
        edge = SUBLANES - 1 if d == 0 else 0

        def tile_body(i, hprev):
            tile = i if d == 0 else n_tiles - 1 - i
            r0 = pl.multiple_of(tile * tl, tl)
            rows = pl.ds(r0, tl)
            if d == 0:
                xc = conv(xr_ref[rows, :])
                xc_scr[rows, :] = xc
            else:
                xc = xc_scr[rows, :]
            xcb = xc.astype(BF16)
            r = _sigmoid(jnp.dot(xcb, rgw_ref[d, 0].astype(BF16), preferred_element_type=F32)
                         + rgb_ref[d:d + 1, :])
            ig = _sigmoid(jnp.dot(xcb, igw_ref[d, 0].astype(BF16), preferred_element_type=F32)
                          + igb_ref[d:d + 1, :])
            log_a = (-LRU_C * r) * sp
            a = jnp.exp(log_a)
            th = jnp.tanh(log_a)
            u = jnp.sqrt((-2.0 * th) / (1.0 - th)) * (ig * xc)
            s = 1
            while s < SUBLANES:
                if d == 0:
                    ok = sub >= s
                    u = u + a * jnp.where(ok, pltpu.roll(u, s, 0), 0.0)
                    a = a * jnp.where(ok, pltpu.roll(a, s, 0), 1.0)
                else:
                    ok = sub < SUBLANES - s
                    u = u + a * jnp.where(ok, pltpu.roll(u, tl - s, 0), 0.0)
                    a = a * jnp.where(ok, pltpu.roll(a, tl - s, 0), 1.0)
                s *= 2
            a_scr[...] = a
            u_scr[...] = u

            def group_body(gi, hp):
                grp = gi if d == 0 else n_groups - 1 - gi
                grows = pl.ds(pl.multiple_of(grp * SUBLANES, SUBLANES), SUBLANES)
                hb = u_scr[grows, :] + a_scr[grows, :] * hp
                h_scr[grows, :] = hb
                return hb[edge:edge + 1, :]

            hlast = lax.fori_loop(0, n_groups, group_body, hprev)
            if d == 0:
                y_scr[rows, :] = h_scr[...]
            else:
                y = y_scr[rows, :] + h_scr[...]
                out_ref[rows, :] = (y * _gelu_tanh(gate_ref[rows, :])).astype(out_ref.dtype)
            return hlast

        h0 = s0_ref[0, 0, d:d + 1, :] if has_s0 else jnp.zeros((1, w), F32)
        st_ref[0, 0, d:d + 1, :] = lax.fori_loop(0, n_tiles, tile_body, h0)

    tile_pass(0)
    tile_pass(1)


def _lru_scan(e, proj, prev, s0, n_seq, seq_len, row_len, row0, p):
    t = seq_len
    tb0 = row0 // t
    has_s0 = s0 is not None
    in_specs = [pl.BlockSpec((t, LRU_BW), lambda s, n: (tb0 + s, n)),
                pl.BlockSpec((t, LRU_BW), lambda s, n: (tb0 + s, LRU_BLOCKS + n)),
                pl.BlockSpec((CONV_W, LRU_BW), lambda s, n: (0, n)),
                pl.BlockSpec((1, LRU_BW), lambda s, n: (0, n)),
                pl.BlockSpec((2, 1, LRU_BW, LRU_BW), lambda s, n: (0, n, 0, 0)),
                pl.BlockSpec((2, LRU_BW), lambda s, n: (0, n)),
                pl.BlockSpec((2, 1, LRU_BW, LRU_BW), lambda s, n: (0, n, 0, 0)),
                pl.BlockSpec((2, LRU_BW), lambda s, n: (0, n)),
                pl.BlockSpec((2, LRU_BW), lambda s, n: (0, n))]
    args = [proj, proj, p["conv_w"], p["conv_b"], p["rg_w"], p["rg_b"], p["ig_w"], p["ig_b"], p["lam"]]
    if has_s0:
        in_specs.append(pl.BlockSpec((1, 1, 2, LRU_BW), lambda s, n: (s, e, 0, n)))
        args.append(s0)
    aliases = {}
    in_specs.append(pl.BlockSpec(memory_space=pl.ANY))
    if prev is None:
        args.append(jnp.zeros((SUBLANES, LANES), BF16))
    else:
        aliases = {len(args): 0}
        args.append(prev)
    return pl.pallas_call(
        functools.partial(_lru_kernel, seq_len=t, row_len=row_len, has_s0=has_s0),
        grid=(n_seq, LRU_BLOCKS),
        in_specs=in_specs,
        out_specs=[pl.BlockSpec((t, LRU_BW), lambda s, n: (tb0 + s, n)),
                   pl.BlockSpec((1, 1, 2, LRU_BW), lambda s, n: (s, 0, 0, n))],
        out_shape=[jax.ShapeDtypeStruct((TOKENS, LRU_WIDTH), BF16),
                   jax.ShapeDtypeStruct((n_seq, 1, 2, LRU_WIDTH), F32)],
        scratch_shapes=[pltpu.VMEM((t, LRU_BW), F32), pltpu.VMEM((t, LRU_BW), F32),
                        pltpu.VMEM((TIME_TILE, LRU_BW), F32), pltpu.VMEM((TIME_TILE, LRU_BW), F32),
                        pltpu.VMEM((TIME_TILE, LRU_BW), F32)],
        input_output_aliases=aliases,
        compiler_params=_params("parallel", "arbitrary"),
        name="lru_scan",
    )(*args)


def kernel(x_prompt, x_sample, state_hgrn, state_gla, state_rglru, c, c_ctx, norm1_g, norm2_g, w_mod, b_mod,
           ffn_w_in, ffn_w_out, hgrn_lower_bounds, even_w_in, gla_w_alpha, gla_b_alpha, hgrn_norm_g,
           gla_norm_g, even_w_out, odd_w_in, conv_w, conv_b, rg_w, rg_b, ig_w, ig_b, lru_lambda, odd_w_out,
           final_norm_g):
    x = jnp.concatenate([x_prompt.reshape(N_PROMPT, D_MODEL), x_sample.reshape(N_SAMPLE, D_MODEL)], axis=0)
    c3 = jnp.concatenate([c_ctx[None, :], c, jnp.zeros((MOD_ROWS - 1 - DEC_BATCH, D_MODEL), F32)], axis=0)
    mod = _modulation(c3, w_mod, b_mod)

    new_hgrn, new_gla, new_lru = [], [], []
    for l in range(DEPTH):
        e = l // 2
        g1 = norm1_g[l][None, :]
        if l % 2 == 0:
            proj, lowrank = _inproj(x, g1, mod, l, even_w_in, e, EVEN_MAIN, even_w_in[e][:, EVEN_MAIN:])
            wa = jnp.zeros((2, 2 * GLA_RANK, B_KDIM), F32)
            wa = wa.at[0, :GLA_RANK].set(gla_w_alpha[e, 0]).at[1, GLA_RANK:].set(gla_w_alpha[e, 1])
            small_a = {"lb": hgrn_lower_bounds, "gain": hgrn_norm_g[e][None, :]}
            small_b = {"wa": wa, "ba": gla_b_alpha[e], "gain": gla_norm_g[e][None, :]}
            merged, sa = _gla_scan("hgrn", l, e, proj, None, None, None, BATCH, SEQ, 0, small_a)
            merged, sb = _gla_scan("gla", l, e, proj, lowrank, merged, None, BATCH, SEQ, 0, small_b)
            merged, _ = _gla_scan("hgrn", l, e, proj, None, merged, state_hgrn, DEC_BATCH, DEC_SEQ, N_PROMPT,
                                  small_a)
            merged, _ = _gla_scan("gla", l, e, proj, lowrank, merged, state_gla, DEC_BATCH, DEC_SEQ, N_PROMPT,
                                  small_b)
            new_hgrn.append(sa)
            new_gla.append(sb)
            w_out = even_w_out
        else:
            proj = _inproj(x, g1, mod, l, odd_w_in, e, 2 * LRU_WIDTH)
            p = {"conv_w": conv_w[e], "conv_b": conv_b[e][None, :], "rg_w": rg_w[e], "rg_b": rg_b[e],
                 "ig_w": ig_w[e], "ig_b": ig_b[e], "lam": lru_lambda[e]}
            merged, sr = _lru_scan(e, proj, None, None, BATCH, SEQ, SEQ, 0, p)
            merged, _ = _lru_scan(e, proj, merged, state_rglru, DEC_BATCH, DEC_SEQ, GRID_W, N_PROMPT, p)
            new_lru.append(sr)
            w_out = odd_w_out
        x = _outproj(merged, w_out, e, x, mod, l, 2, 1024, 1024)
        hidden = _ffn_up(x, norm2_g[l][None, :], mod, l, ffn_w_in)
        x = _outproj(hidden, ffn_w_out, l, x, mod, l, 5, 1024, 256)

    gf = final_norm_g[None, :]
    y_prompt = _final_norm(x, gf, 0, N_PROMPT)
    y_sample = _final_norm(x, gf, N_PROMPT, N_SAMPLE)
    dt = x_prompt.dtype
    return (y_prompt.reshape(BATCH, SEQ, D_MODEL), y_sample.reshape(DEC_BATCH, DEC_SEQ, D_MODEL),
            jnp.concatenate(new_hgrn, axis=1).astype(dt), jnp.concatenate(new_gla, axis=1).astype(dt),
            jnp.concatenate(new_lru, axis=1).astype(dt))
```

```python
import functools

import jax
import jax.numpy as jnp
from jax import lax
from jax.experimental import pallas as pl
from jax.experimental.pallas import tpu as pltpu

F32 = jnp.float32
BF16 = jnp.bfloat16

D_MODEL = 2048
BATCH = 16
SEQ = 256
DEPTH = 2
DEC_BATCH = 2
DEC_SEQ = 2048
GRID_W = 64
N_EVEN = (DEPTH + 1) // 2
N_ODD = DEPTH // 2
A_HEADS = 8
A_DK = 128
A_DV = 128
A_KDIM = A_HEADS * A_DK
A_VDIM = A_HEADS * A_DV
B_HEADS = 4
B_DK = 128
B_DV = 256
B_KDIM = B_HEADS * B_DK
B_VDIM = B_HEADS * B_DV
GLA_RANK = 16
GLA_GATE_NORM = 16.0
CHUNK = 32
EVEN_MAIN = 3 * A_KDIM + 2 * A_VDIM + 2 * B_KDIM + 2 * B_VDIM
LRU_WIDTH = D_MODEL
LRU_BLOCKS = 8
LRU_BW = LRU_WIDTH // LRU_BLOCKS
LRU_C = 8.0
CONV_W = 4
D_FF = (8 * D_MODEL + 3 * 256 - 1) // (3 * 256) * 256
EPS = 1e-6

N_PROMPT = BATCH * SEQ
N_SAMPLE = DEC_BATCH * DEC_SEQ
TOKENS = N_PROMPT + N_SAMPLE
MOD_ROWS = 8
SUBLANES = 8
LANES = 128
TIME_TILE = 256
VMEM_LIMIT = 56 * 1024 * 1024


def _sigmoid(x):
    return 0.5 * (jnp.tanh(0.5 * x) + 1.0)


def _silu(x):
    return x * _sigmoid(x)


def _softplus(x):
    return jnp.maximum(x, 0.0) + jnp.log1p(jnp.exp(-jnp.abs(x)))


def _gelu_tanh(x):
    c = 0.7978845608028654
    return x * (0.5 * (1.0 + jnp.tanh(c * (x + 0.044715 * (x * x * x)))))


def _mod_row(tile, tile_rows):
    r0 = tile * tile_rows
    return jnp.where(r0 < N_PROMPT, 0, 1 + (r0 - N_PROMPT) // DEC_SEQ)


def _mod_spec(layer, which, tile_rows, width=D_MODEL):
    base = (layer * 6 + which) * MOD_ROWS
    if width == D_MODEL:
        return pl.BlockSpec((1, 1, width), lambda i, j: (base + _mod_row(i, tile_rows), 0, 0))
    return pl.BlockSpec((1, 1, width), lambda i, j: (base + _mod_row(i, tile_rows), 0, j))


def _params(*sem):
    return pltpu.CompilerParams(dimension_semantics=sem, vmem_limit_bytes=VMEM_LIMIT)


def _mod_kernel(c_ref, w_ref, b_ref, o_ref):
    s = _silu(c_ref[...]).astype(BF16)
    o_ref[0, 0] = jnp.dot(s, w_ref[0].astype(BF16), preferred_element_type=F32) + b_ref[0]


def _modulation(c3, w_mod, b_mod):
    tn = 1024
    nb = D_MODEL // tn
    out = pl.pallas_call(
        _mod_kernel,
        grid=(DEPTH, 6, nb),
        in_specs=[pl.BlockSpec((MOD_ROWS, D_MODEL), lambda l, k, j: (0, 0)),
                  pl.BlockSpec((1, D_MODEL, tn), lambda l, k, j: (l, 0, k * nb + j)),
                  pl.BlockSpec((1, 1, tn), lambda l, k, j: (l, 0, k * nb + j))],
        out_specs=pl.BlockSpec((1, 1, MOD_ROWS, tn), lambda l, k, j: (l, k, 0, j)),
        out_shape=jax.ShapeDtypeStruct((DEPTH, 6, MOD_ROWS, D_MODEL), F32),
        compiler_params=_params("arbitrary", "arbitrary", "arbitrary"),
        name="modulation",
    )(c3, w_mod, b_mod.reshape(DEPTH, 1, 6 * D_MODEL))
    return out.reshape(DEPTH * 6 * MOD_ROWS, 1, D_MODEL)


def _norm_mod(x, g, sc, sh):
    ms = jnp.mean(x * x, axis=-1, keepdims=True)
    y = (x * lax.rsqrt(ms + EPS)) * g
    return (y * (1.0 + sc) + sh).astype(BF16)


def _inproj_kernel(x_ref, g_ref, sc_ref, sh_ref, w_ref, o_ref, h_scr):
    @pl.when(pl.program_id(1) == 0)
    def _():
        h_scr[...] = _norm_mod(x_ref[...], g_ref[...], sc_ref[0], sh_ref[0])

    o_ref[...] = jnp.dot(h_scr[...], w_ref[...].astype(BF16), preferred_element_type=F32)


def _inproj_lowrank_kernel(x_ref, g_ref, sc_ref, sh_ref, w_ref, wl_ref, o_ref, ol_ref, h_scr):
    @pl.when(pl.program_id(1) == 0)
    def _():
        h = _norm_mod(x_ref[...], g_ref[...], sc_ref[0], sh_ref[0])
        h_scr[...] = h
        ol_ref[...] = jnp.dot(h, wl_ref[...].astype(BF16), preferred_element_type=F32)

    o_ref[...] = jnp.dot(h_scr[...], w_ref[...].astype(BF16), preferred_element_type=F32)


def _inproj(x, g, mod, layer, w, widx, n_main, w_low=None):
    tm, tn = 1024, 512
    grid = (TOKENS // tm, n_main // tn)
    in_specs = [pl.BlockSpec((tm, D_MODEL), lambda i, j: (i, 0)),
                pl.BlockSpec((1, D_MODEL), lambda i, j: (0, 0)),
                _mod_spec(layer, 1, tm),
                _mod_spec(layer, 0, tm),
                pl.BlockSpec((None, D_MODEL, tn), lambda i, j: (widx, 0, j))]
    out_specs = pl.BlockSpec((tm, tn), lambda i, j: (i, j))
    out_shape = jax.ShapeDtypeStruct((TOKENS, n_main), F32)
    args = [x, g, mod, mod, w]
    body = _inproj_kernel
    if w_low is not None:
        nl = w_low.shape[1]
        in_specs.append(pl.BlockSpec((D_MODEL, nl), lambda i, j: (0, 0)))
        out_specs = [out_specs, pl.BlockSpec((tm, nl), lambda i, j: (i, 0))]
        out_shape = [out_shape, jax.ShapeDtypeStruct((TOKENS, nl), F32)]
        args.append(w_low)
        body = _inproj_lowrank_kernel
    return pl.pallas_call(
        body, grid=grid, in_specs=in_specs, out_specs=out_specs, out_shape=out_shape,
        scratch_shapes=[pltpu.VMEM((tm, D_MODEL), BF16)],
        compiler_params=_params("parallel", "arbitrary"),
        name="inproj",
    )(*args)


def _outproj_kernel(a_ref, w_ref, x_ref, gate_ref, o_ref):
    acc = jnp.dot(a_ref[...], w_ref[...].astype(BF16), preferred_element_type=F32)
    o_ref[...] = x_ref[...] + gate_ref[0] * acc


def _outproj(a, w, widx, x, mod, layer, which, tm, tn):
    k = a.shape[1]
    return pl.pallas_call(
        _outproj_kernel, grid=(TOKENS // tm, D_MODEL // tn),
        in_specs=[pl.BlockSpec((tm, k), lambda i, j: (i, 0)),
                  pl.BlockSpec((None, k, tn), lambda i, j: (widx, 0, j)),
                  pl.BlockSpec((tm, tn), lambda i, j: (i, j)),
                  _mod_spec(layer, which, tm, tn)],
        out_specs=pl.BlockSpec((tm, tn), lambda i, j: (i, j)),
        out_shape=jax.ShapeDtypeStruct((TOKENS, D_MODEL), F32),
        input_output_aliases={2: 0},
        compiler_params=_params("parallel", "arbitrary"),
        name="outproj",
    )(a, w, x, mod)


def _ffn_up_kernel(x_ref, g_ref, sc_ref, sh_ref, wg_ref, wu_ref, o_ref, h_scr):
    @pl.when(pl.program_id(1) == 0)
    def _():
        h_scr[...] = _norm_mod(x_ref[...], g_ref[...], sc_ref[0], sh_ref[0])

    h = h_scr[...]
    gt = jnp.dot(h, wg_ref[...].astype(BF16), preferred_element_type=F32)
    up = jnp.dot(h, wu_ref[...].astype(BF16), preferred_element_type=F32)
    o_ref[...] = (_silu(gt) * up).astype(BF16)


def _ffn_up(x, g, mod, layer, w_in):
    tm, tf = 1024, 512
    nf = D_FF // tf
    return pl.pallas_call(
        _ffn_up_kernel, grid=(TOKENS // tm, nf),
        in_specs=[pl.BlockSpec((tm, D_MODEL), lambda i, j: (i, 0)),
                  pl.BlockSpec((1, D_MODEL), lambda i, j: (0, 0)),
                  _mod_spec(layer, 4, tm),
                  _mod_spec(layer, 3, tm),
                  pl.BlockSpec((None, D_MODEL, tf), lambda i, j: (layer, 0, j)),
                  pl.BlockSpec((None, D_MODEL, tf), lambda i, j: (layer, 0, nf + j))],
        out_specs=pl.BlockSpec((tm, tf), lambda i, j: (i, j)),
        out_shape=jax.ShapeDtypeStruct((TOKENS, D_FF), BF16),
        scratch_shapes=[pltpu.VMEM((tm, D_MODEL), BF16)],
        compiler_params=_params("parallel", "arbitrary"),
        name="ffn_up",
    )(x, g, mod, mod, w_in, w_in)


def _final_norm_kernel(x_ref, g_ref, o_ref):
    x = x_ref[...]
    ms = jnp.mean(x * x, axis=-1, keepdims=True)
    o_ref[...] = (x * lax.rsqrt(ms + EPS)) * g_ref[...]


def _final_norm(x, g, row0, rows):
    tm = 512
    t0 = row0 // tm
    return pl.pallas_call(
        _final_norm_kernel, grid=(rows // tm,),
        in_specs=[pl.BlockSpec((tm, D_MODEL), lambda i: (t0 + i, 0)),
                  pl.BlockSpec((1, D_MODEL), lambda i: (0, 0))],
        out_specs=pl.BlockSpec((tm, D_MODEL), lambda i: (i, 0)),
        out_shape=jax.ShapeDtypeStruct((rows, D_MODEL), F32),
        compiler_params=_params("parallel"),
        name="final_norm",
    )(x, g)


def _chunk_cumsum(g, pos, reverse):
    x = g
    s = 1
    while s < CHUNK:
        if reverse:
            x = x + jnp.where(pos < CHUNK - s, pltpu.roll(x, CHUNK - s, 0), 0.0)
        else:
            x = x + jnp.where(pos >= s, pltpu.roll(x, s, 0), 0.0)
        s *= 2
    return x


def _gla_kernel(*refs, variant, layer, n_seq, seq_len, hb, dv, has_s0):
    refs = list(refs)
    if variant == "hgrn":
        q_ref, ff_ref, fb_ref, v_ref, gate_ref, lb_ref, gain_ref = refs[:7]
        refs = refs[7:]
        gate_in = (ff_ref, fb_ref)
    else:
        q_ref, k_ref, v_ref, gate_ref, lr_ref, wa_ref, ba_ref, gain_ref = refs[:8]
        refs = refs[8:]
    s0_ref = refs.pop(0) if has_s0 else None
    refs.pop(0)
    out_ref, st_ref = refs[:2]
    o_scr, s_scr = refs[2:4]
    g_scr = refs[4:]

    t = seq_len
    n = t // CHUNK
    pos = lax.broadcasted_iota(jnp.int32, (CHUNK, A_DK), 0)
    ri = lax.broadcasted_iota(jnp.int32, (CHUNK, CHUNK), 0)
    ci = lax.broadcasted_iota(jnp.int32, (CHUNK, CHUNK), 1)
    masks = (ci <= ri, ci >= ri)
    nt_dims = (((1,), (1,)), ((), ()))
    tn_dims = (((0,), (0,)), ((), ()))
    gain = gain_ref[...]

    rows_total = n_seq * t
    if variant == "hgrn":
        lbr = lb_ref[...]
        ex = jnp.exp(lbr - jnp.max(lbr, axis=0, keepdims=True))
        lb = jnp.sum(ex[:layer + 1], axis=0, keepdims=True) / jnp.sum(ex, axis=0, keepdims=True)
    else:
        def decay_tile(i, carry):
            rows = pl.ds(pl.multiple_of(i * TIME_TILE, TIME_TILE), TIME_TILE)
            lr = lr_ref[rows, :].astype(BF16)
            for d in range(2):
                z = jnp.dot(lr, wa_ref[d].astype(BF16), preferred_element_type=F32) + ba_ref[d:d + 1, :]
                g_scr[d][rows, :] = -_softplus(-z) / GLA_GATE_NORM
            return carry

        lax.fori_loop(0, rows_total // TIME_TILE, decay_tile, 0)

    def chunk_step(c, s, h, d):
        rows = pl.ds(pl.multiple_of(s * t + c * CHUNK, CHUNK), CHUNK)
        kc = slice(h * A_DK, (h + 1) * A_DK)
        vc = slice(h * dv, (h + 1) * dv)
        chain = (s * hb + h) * 2 + d
        q = q_ref[rows, kc] * (A_DK ** -0.5)
        if variant == "hgrn":
            lbh = lb[:, kc]
            f = lbh + (1.0 - lbh) * _sigmoid(gate_in[d][rows, kc])
            k = 1.0 - f
            g = jnp.log(f)
        else:
            k = k_ref[rows, kc]
            g = g_scr[d][rows, kc]
        b = _chunk_cumsum(g, pos, d == 1)
        tot = b[0:1, :] if d == 1 else b[CHUNK - 1:CHUNK, :]
        qd = (q * jnp.exp(b)).astype(BF16)
        ki = (k * jnp.exp(-b)).astype(BF16)
        kt = (k * jnp.exp(tot - b)).astype(BF16)
        vb = v_ref[rows, vc].astype(BF16)
        s_t = s_scr[chain]
        att = lax.dot_general(qd, ki, nt_dims, preferred_element_type=F32)
        att = jnp.where(masks[d], att, 0.0).astype(BF16)
        o = (lax.dot_general(qd, s_t.astype(BF16), nt_dims, preferred_element_type=F32)
             + jnp.dot(att, vb, preferred_element_type=F32))
        s_scr[chain] = jnp.exp(tot) * s_t + lax.dot_general(vb, kt, tn_dims, preferred_element_type=F32)
        return rows, vc, o

    def finish(rows, vc, o):
        o = o + o_scr[rows, vc]
        ms = jnp.mean(o * o, axis=-1, keepdims=True)
        o = (o * lax.rsqrt(ms + EPS)) * gain
        out_ref[rows, vc] = (o * _silu(gate_ref[rows, vc])).astype(out_ref.dtype)

    def first_half(i, carry):
        for s in range(n_seq):
            for h in range(hb):
                for d in range(2):
                    rows, vc, o = chunk_step(i if d == 0 else n - 1 - i, s, h, d)
                    o_scr[rows, vc] = o
        return carry

    def second_half(i, carry):
        for s in range(n_seq):
            for h in range(hb):
                for d in range(2):
                    finish(*chunk_step(i if d == 0 else n - 1 - i, s, h, d))
        return carry

    for s in range(n_seq):
        for h in range(hb):
            for d in range(2):
                chain = (s * hb + h) * 2 + d
                s_scr[chain] = s0_ref[s, 0, d, h].T if has_s0 else jnp.zeros((dv, A_DK), F32)
    lax.fori_loop(0, n // 2, first_half, 0)
    lax.fori_loop(n // 2, n, second_half, 0)
    for s in range(n_seq):
        for h in range(hb):
            for d in range(2):
                st_ref[s, 0, d, h] = s_scr[(s * hb + h) * 2 + d].T


def _gla_scan(variant, layer, e, proj, lowrank, prev, s0, n_seq, seq_len, row0, small, seq_blk, hb):
    t = seq_len
    rows = seq_blk * t
    tb0 = row0 // rows
    heads, dv = (A_HEADS, A_DV) if variant == "hgrn" else (B_HEADS, B_DV)

    def col_spec(width, col0):
        blk0 = col0 // (hb * width)
        return pl.BlockSpec((rows, hb * width), lambda s, h: (tb0 + s, blk0 + h))

    if variant == "hgrn":
        o_q, o_ff, o_fb, o_v, o_g = 0, A_KDIM, 2 * A_KDIM, 3 * A_KDIM, 3 * A_KDIM + A_VDIM
        in_specs = [col_spec(A_DK, o_q), col_spec(A_DK, o_ff), col_spec(A_DK, o_fb),
                    col_spec(dv, o_v), col_spec(dv, o_g),
                    pl.BlockSpec((DEPTH + 1, hb * A_DK), lambda s, h: (0, h)),
                    pl.BlockSpec((1, dv), lambda s, h: (0, 0))]
        args = [proj] * 5 + [small["lb"], small["gain"]]
        out_col0 = 0
        decay_scratch = []
    else:
        base = 3 * A_KDIM + 2 * A_VDIM
        o_q, o_k, o_v, o_g = base, base + B_KDIM, base + 2 * B_KDIM, base + 2 * B_KDIM + B_VDIM
        in_specs = [col_spec(B_DK, o_q), col_spec(B_DK, o_k),
                    col_spec(dv, o_v), col_spec(dv, o_g),
                    pl.BlockSpec((rows, 2 * GLA_RANK), lambda s, h: (tb0 + s, 0)),
                    pl.BlockSpec((2, 2 * GLA_RANK, hb * B_DK), lambda s, h: (0, 0, h)),
                    pl.BlockSpec((2, hb * B_DK), lambda s, h: (0, h)),
                    pl.BlockSpec((1, dv), lambda s, h: (0, 0))]
        args = [proj] * 4 + [lowrank, small["wa"], small["ba"], small["gain"]]
        out_col0 = A_VDIM
        decay_scratch = [pltpu.VMEM((rows, hb * B_DK), F32)] * 2
    has_s0 = s0 is not None
    if has_s0:
        in_specs.append(pl.BlockSpec((seq_blk, 1, 2, hb, A_DK, dv), lambda s, h: (s, e, 0, h, 0, 0)))
        args.append(s0)
    aliases = {}
    in_specs.append(pl.BlockSpec(memory_space=pl.ANY))
    if prev is None:
        args.append(jnp.zeros((SUBLANES, LANES), BF16))
    else:
        aliases = {len(args): 0}
        args.append(prev)
    out_blk0 = out_col0 // (hb * dv)
    merged, states = pl.pallas_call(
        functools.partial(_gla_kernel, variant=variant, layer=layer, n_seq=seq_blk, seq_len=t, hb=hb, dv=dv,
                          has_s0=has_s0),
        grid=(n_seq // seq_blk, heads // hb),
        in_specs=in_specs,
        out_specs=[pl.BlockSpec((rows, hb * dv), lambda s, h: (tb0 + s, out_blk0 + h)),
                   pl.BlockSpec((seq_blk, 1, 2, hb, A_DK, dv), lambda s, h: (s, 0, 0, h, 0, 0))],
        out_shape=[jax.ShapeDtypeStruct((TOKENS, A_VDIM + B_VDIM), BF16),
                   jax.ShapeDtypeStruct((n_seq, 1, 2, heads, A_DK, dv), F32)],
        scratch_shapes=[pltpu.VMEM((rows, hb * dv), F32), pltpu.VMEM((seq_blk * hb * 2, dv, A_DK), F32)]
                       + decay_scratch,
        input_output_aliases=aliases,
        compiler_params=_params("parallel", "arbitrary"),
        name=f"{variant}_scan",
    )(*args)
    return merged, states


def _lru_kernel(*refs, seq_len, row_len, has_s0):
    refs = list(refs)
    xr_ref, gate_ref, cw_ref, cb_ref, rgw_ref, rgb_ref, igw_ref, igb_ref, lam_ref = refs[:9]
    refs = refs[9:]
    s0_ref = refs.pop(0) if has_s0 else None
    refs.pop(0)
    out_ref, st_ref = refs[:2]
    xc_scr, y_scr, a_scr, u_scr, h_scr = refs[2:]

    t = seq_len
    w = LRU_BW
    tl = TIME_TILE
    n_tiles = t // tl
    n_groups = tl // SUBLANES
    pos = lax.broadcasted_iota(jnp.int32, (tl, w), 0) % row_len
    sub = lax.broadcasted_iota(jnp.int32, (n_groups, SUBLANES, w), 1)
    cw = cw_ref[...]

    def conv(x):
        xc = cb_ref[...] + jnp.where(pos >= 2, pltpu.roll(x, 2, 0), 0.0) * cw[0:1, :]
        xc = xc + jnp.where(pos >= 1, pltpu.roll(x, 1, 0), 0.0) * cw[1:2, :]
        xc = xc + x * cw[2:3, :]
        return xc + jnp.where(pos < row_len - 1, pltpu.roll(x, tl - 1, 0), 0.0) * cw[3:4, :]

    def tile_pass(d):
        sp = _softplus(-lam_ref[d:d + 1, :])
        edge = SUBLANES - 1 if d == 0 else 0

        def tile_body(i, hprev):
            tile = i if d == 0 else n_tiles - 1 - i
            r0 = pl.multiple_of(tile * tl, tl)
            rows = pl.ds(r0, tl)
            if d == 0:
                xc = conv(xr_ref[rows, :])
                xc_scr[rows, :] = xc
            else:
                xc = xc_scr[rows, :]
            xcb = xc.astype(BF16)
            r = _sigmoid(jnp.dot(xcb, rgw_ref[d, 0].astype(BF16), preferred_element_type=F32)
                         + rgb_ref[d:d + 1, :])
            ig = _sigmoid(jnp.dot(xcb, igw_ref[d, 0].astype(BF16), preferred_element_type=F32)
                          + igb_ref[d:d + 1, :])
            log_a = (-LRU_C * r) * sp
            a = jnp.exp(log_a)
            u = jnp.sqrt(-jnp.tanh(log_a) * (a * a + 1.0)) * (ig * xc)
            a = a.reshape(n_groups, SUBLANES, w)
            u = u.reshape(n_groups, SUBLANES, w)
            s = 1
            while s < SUBLANES:
                if d == 0:
                    ok = sub >= s
                    u = u + a * jnp.where(ok, pltpu.roll(u, s, 1), 0.0)
                    a = a * jnp.where(ok, pltpu.roll(a, s, 1), 1.0)
                else:
                    ok = sub < SUBLANES - s
                    u = u + a * jnp.where(ok, pltpu.roll(u, SUBLANES - s, 1), 0.0)
                    a = a * jnp.where(ok, pltpu.roll(a, SUBLANES - s, 1), 1.0)
                s *= 2
            a_scr[...] = a.reshape(tl, w)
            u_scr[...] = u.reshape(tl, w)

            def group_body(gi, hp):
                grp = gi if d == 0 else n_groups - 1 - gi
                grows = pl.ds(pl.multiple_of(grp * SUBLANES, SUBLANES), SUBLANES)
                hb = u_scr[grows, :] + a_scr[grows, :] * hp
                h_scr[grows, :] = hb
                return hb[edge:edge + 1, :]

            hlast = lax.fori_loop(0, n_groups, group_body, hprev)
            if d == 0:
                y_scr[rows, :] = h_scr[...]
            else:
                y = y_scr[rows, :] + h_scr[...]
                out_ref[rows, :] = (y * _gelu_tanh(gate_ref[rows, :])).astype(out_ref.dtype)
            return hlast

        h0 = s0_ref[0, 0, d:d + 1, :] if has_s0 else jnp.zeros((1, w), F32)
        st_ref[0, 0, d:d + 1, :] = lax.fori_loop(0, n_tiles, tile_body, h0)

    tile_pass(0)
    tile_pass(1)


def _lru_scan(e, proj, prev, s0, n_seq, seq_len, row_len, row0, p):
    t = seq_len
    tb0 = row0 // t
    has_s0 = s0 is not None
    in_specs = [pl.BlockSpec((t, LRU_BW), lambda s, n: (tb0 + s, n)),
                pl.BlockSpec((t, LRU_BW), lambda s, n: (tb0 + s, LRU_BLOCKS + n)),
                pl.BlockSpec((CONV_W, LRU_BW), lambda s, n: (0, n)),
                pl.BlockSpec((1, LRU_BW), lambda s, n: (0, n)),
                pl.BlockSpec((2, 1, LRU_BW, LRU_BW), lambda s, n: (0, n, 0, 0)),
                pl.BlockSpec((2, LRU_BW), lambda s, n: (0, n)),
                pl.BlockSpec((2, 1, LRU_BW, LRU_BW), lambda s, n: (0, n, 0, 0)),
                pl.BlockSpec((2, LRU_BW), lambda s, n: (0, n)),
                pl.BlockSpec((2, LRU_BW), lambda s, n: (0, n))]
    args = [proj, proj, p["conv_w"], p["conv_b"], p["rg_w"], p["rg_b"], p["ig_w"], p["ig_b"], p["lam"]]
    if has_s0:
        in_specs.append(pl.BlockSpec((1, 1, 2, LRU_BW), lambda s, n: (s, e, 0, n)))
        args.append(s0)
    aliases = {}
    in_specs.append(pl.BlockSpec(memory_space=pl.ANY))
    if prev is None:
        args.append(jnp.zeros((SUBLANES, LANES), BF16))
    else:
        aliases = {len(args): 0}
        args.append(prev)
    return pl.pallas_call(
        functools.partial(_lru_kernel, seq_len=t, row_len=row_len, has_s0=has_s0),
        grid=(n_seq, LRU_BLOCKS),
        in_specs=in_specs,
        out_specs=[pl.BlockSpec((t, LRU_BW), lambda s, n: (tb0 + s, n)),
                   pl.BlockSpec((1, 1, 2, LRU_BW), lambda s, n: (s, 0, 0, n))],
        out_shape=[jax.ShapeDtypeStruct((TOKENS, LRU_WIDTH), BF16),
                   jax.ShapeDtypeStruct((n_seq, 1, 2, LRU_WIDTH), F32)],
        scratch_shapes=[pltpu.VMEM((t, LRU_BW), F32), pltpu.VMEM((t, LRU_BW), F32),
                        pltpu.VMEM((TIME_TILE, LRU_BW), F32), pltpu.VMEM((TIME_TILE, LRU_BW), F32),
                        pltpu.VMEM((TIME_TILE, LRU_BW), F32)],
        input_output_aliases=aliases,
        compiler_params=_params("parallel", "arbitrary"),
        name="lru_scan",
    )(*args)


def kernel(x_prompt, x_sample, state_hgrn, state_gla, state_rglru, c, c_ctx, norm1_g, norm2_g, w_mod, b_mod,
           ffn_w_in, ffn_w_out, hgrn_lower_bounds, even_w_in, gla_w_alpha, gla_b_alpha, hgrn_norm_g,
           gla_norm_g, even_w_out, odd_w_in, conv_w, conv_b, rg_w, rg_b, ig_w, ig_b, lru_lambda, odd_w_out,
           final_norm_g):
    x = jnp.concatenate([x_prompt.reshape(N_PROMPT, D_MODEL), x_sample.reshape(N_SAMPLE, D_MODEL)], axis=0)
    c3 = jnp.concatenate([c_ctx[None, :], c, jnp.zeros((MOD_ROWS - 1 - DEC_BATCH, D_MODEL), F32)], axis=0)
    mod = _modulation(c3, w_mod, b_mod)

    new_hgrn, new_gla, new_lru = [], [], []
    for l in range(DEPTH):
        e = l // 2
        g1 = norm1_g[l][None, :]
        if l % 2 == 0:
            proj, lowrank = _inproj(x, g1, mod, l, even_w_in, e, EVEN_MAIN, even_w_in[e][:, EVEN_MAIN:])
            wa = jnp.zeros((2, 2 * GLA_RANK, B_KDIM), F32)
            wa = wa.at[0, :GLA_RANK].set(gla_w_alpha[e, 0]).at[1, GLA_RANK:].set(gla_w_alpha[e, 1])
            small_a = {"lb": hgrn_lower_bounds, "gain": hgrn_norm_g[e][None, :]}
            small_b = {"wa": wa, "ba": gla_b_alpha[e], "gain": gla_norm_g[e][None, :]}
            merged, sa = _gla_scan("hgrn", l, e, proj, None, None, None, BATCH, SEQ, 0, small_a, 4, 1)
            merged, sb = _gla_scan("gla", l, e, proj, lowrank, merged, None, BATCH, SEQ, 0, small_b, 4, 1)
            merged, _ = _gla_scan("hgrn", l, e, proj, None, merged, state_hgrn, DEC_BATCH, DEC_SEQ, N_PROMPT,
                                  small_a, 2, 2)
            merged, _ = _gla_scan("gla", l, e, proj, lowrank, merged, state_gla, DEC_BATCH, DEC_SEQ, N_PROMPT,
                                  small_b, 2, 1)
            new_hgrn.append(sa)
            new_gla.append(sb)
            w_out = even_w_out
        else:
            proj = _inproj(x, g1, mod, l, odd_w_in, e, 2 * LRU_WIDTH)
            p = {"conv_w": conv_w[e], "conv_b": conv_b[e][None, :], "rg_w": rg_w[e], "rg_b": rg_b[e],
                 "ig_w": ig_w[e], "ig_b": ig_b[e], "lam": lru_lambda[e]}
            merged, sr = _lru_scan(e, proj, None, None, BATCH, SEQ, SEQ, 0, p)
            merged, _ = _lru_scan(e, proj, merged, state_rglru, DEC_BATCH, DEC_SEQ, GRID_W, N_PROMPT, p)
            new_lru.append(sr)
            w_out = odd_w_out
        x = _outproj(merged, w_out, e, x, mod, l, 2, 1024, 1024)
        hidden = _ffn_up(x, norm2_g[l][None, :], mod, l, ffn_w_in)
        x = _outproj(hidden, ffn_w_out, l, x, mod, l, 5, 1024, 256)

    gf = final_norm_g[None, :]
    y_prompt = _final_norm(x, gf, 0, N_PROMPT)
    y_sample = _final_norm(x, gf, N_PROMPT, N_SAMPLE)
    dt = x_prompt.dtype
    return (y_prompt.reshape(BATCH, SEQ, D_MODEL), y_sample.reshape(DEC_BATCH, DEC_SEQ, D_MODEL),
            jnp.concatenate(new_hgrn, axis=1).astype(dt), jnp.concatenate(new_gla, axis=1).astype(dt),
            jnp.concatenate(new_lru, axis=1).astype(dt))
```

```python
import functools

import jax
import jax.numpy as jnp
from jax import lax
from jax.experimental import pallas as pl
from jax.experimental.pallas import tpu as pltpu

F32 = jnp.float32
BF16 = jnp.bfloat16

D_MODEL = 2048
BATCH = 16
SEQ = 256
DEPTH = 2
DEC_BATCH = 2
DEC_SEQ = 2048
GRID_W = 64
N_EVEN = (DEPTH + 1) // 2
N_ODD = DEPTH // 2
A_HEADS = 8
A_DK = 128
A_DV = 128
A_KDIM = A_HEADS * A_DK
A_VDIM = A_HEADS * A_DV
B_HEADS = 4
B_DK = 128
B_DV = 256
B_KDIM = B_HEADS * B_DK
B_VDIM = B_HEADS * B_DV
GLA_RANK = 16
GLA_GATE_NORM = 16.0
SCAN_CHUNK = 64
EVEN_MAIN = 3 * A_KDIM + 2 * A_VDIM + 2 * B_KDIM + 2 * B_VDIM
LRU_WIDTH = D_MODEL
LRU_BLOCKS = 8
LRU_BW = LRU_WIDTH // LRU_BLOCKS
LRU_C = 8.0
CONV_W = 4
D_FF = (8 * D_MODEL + 3 * 256 - 1) // (3 * 256) * 256
EPS = 1e-6

N_PROMPT = BATCH * SEQ
N_SAMPLE = DEC_BATCH * DEC_SEQ
TOKENS = N_PROMPT + N_SAMPLE
MOD_ROWS = 8
SUBLANES = 8
LANES = 128
TIME_TILE = 256
VMEM_LIMIT = 56 * 1024 * 1024


def _sigmoid(x):
    return 0.5 * (jnp.tanh(0.5 * x) + 1.0)


def _silu(x):
    return x * _sigmoid(x)


def _softplus(x):
    return jnp.maximum(x, 0.0) + jnp.log1p(jnp.exp(-jnp.abs(x)))


def _gelu_tanh(x):
    c = 0.7978845608028654
    return x * (0.5 * (1.0 + jnp.tanh(c * (x + 0.044715 * (x * x * x)))))


def _mod_row(tile, tile_rows):
    r0 = tile * tile_rows
    return jnp.where(r0 < N_PROMPT, 0, 1 + (r0 - N_PROMPT) // DEC_SEQ)


def _mod_spec(layer, which, tile_rows, width=D_MODEL):
    base = (layer * 6 + which) * MOD_ROWS
    if width == D_MODEL:
        return pl.BlockSpec((1, 1, width), lambda i, j: (base + _mod_row(i, tile_rows), 0, 0))
    return pl.BlockSpec((1, 1, width), lambda i, j: (base + _mod_row(i, tile_rows), 0, j))


def _params(*sem):
    return pltpu.CompilerParams(dimension_semantics=sem, vmem_limit_bytes=VMEM_LIMIT)


def _stack_kernel(p_ref, s_ref, o_ref, *, prompt_tiles):
    @pl.when(pl.program_id(0) < prompt_tiles)
    def _():
        o_ref[...] = p_ref[...]

    @pl.when(pl.program_id(0) >= prompt_tiles)
    def _():
        o_ref[...] = s_ref[...]


def _stack_tokens(xp, xs):
    tm = 512
    pt = N_PROMPT // tm
    return pl.pallas_call(
        functools.partial(_stack_kernel, prompt_tiles=pt), grid=(TOKENS // tm,),
        in_specs=[pl.BlockSpec((tm, D_MODEL), lambda i: (jnp.minimum(i, pt - 1), 0)),
                  pl.BlockSpec((tm, D_MODEL), lambda i: (jnp.maximum(i - pt, 0), 0))],
        out_specs=pl.BlockSpec((tm, D_MODEL), lambda i: (i, 0)),
        out_shape=jax.ShapeDtypeStruct((TOKENS, D_MODEL), F32),
        compiler_params=_params("arbitrary"),
        name="stack_tokens",
    )(xp, xs)


def _mod_kernel(c_ref, w_ref, b_ref, o_ref):
    s = _silu(c_ref[...]).astype(BF16)
    o_ref[0, 0] = jnp.dot(s, w_ref[0].astype(BF16), preferred_element_type=F32) + b_ref[0]


def _modulation(c3, w_mod, b_mod):
    tn = 1024
    nb = D_MODEL // tn
    out = pl.pallas_call(
        _mod_kernel,
        grid=(DEPTH, 6, nb),
        in_specs=[pl.BlockSpec((MOD_ROWS, D_MODEL), lambda l, k, j: (0, 0)),
                  pl.BlockSpec((1, D_MODEL, tn), lambda l, k, j: (l, 0, k * nb + j)),
                  pl.BlockSpec((1, 1, tn), lambda l, k, j: (l, 0, k * nb + j))],
        out_specs=pl.BlockSpec((1, 1, MOD_ROWS, tn), lambda l, k, j: (l, k, 0, j)),
        out_shape=jax.ShapeDtypeStruct((DEPTH, 6, MOD_ROWS, D_MODEL), F32),
        compiler_params=_params("arbitrary", "arbitrary", "arbitrary"),
        name="modulation",
    )(c3, w_mod, b_mod.reshape(DEPTH, 1, 6 * D_MODEL))
    return out.reshape(DEPTH * 6 * MOD_ROWS, 1, D_MODEL)


def _norm_mod(x, g, sc, sh):
    ms = jnp.mean(x * x, axis=-1, keepdims=True)
    y = (x * lax.rsqrt(ms + EPS)) * g
    return (y * (1.0 + sc) + sh).astype(BF16)


def _inproj_kernel(x_ref, g_ref, sc_ref, sh_ref, w_ref, o_ref, h_scr):
    @pl.when(pl.program_id(1) == 0)
    def _():
        h_scr[...] = _norm_mod(x_ref[...], g_ref[...], sc_ref[0], sh_ref[0])

    o_ref[...] = jnp.dot(h_scr[...], w_ref[...].astype(BF16), preferred_element_type=F32)


def _inproj_lowrank_kernel(x_ref, g_ref, sc_ref, sh_ref, w_ref, wl_ref, o_ref, ol_ref, h_scr):
    @pl.when(pl.program_id(1) == 0)
    def _():
        h = _norm_mod(x_ref[...], g_ref[...], sc_ref[0], sh_ref[0])
        h_scr[...] = h
        ol_ref[...] = jnp.dot(h, wl_ref[...].astype(BF16), preferred_element_type=F32)

    o_ref[...] = jnp.dot(h_scr[...], w_ref[...].astype(BF16), preferred_element_type=F32)


def _inproj(x, g, mod, layer, w, widx, n_main, w_low=None):
    tm, tn = 1024, 512
    grid = (TOKENS // tm, n_main // tn)
    in_specs = [pl.BlockSpec((tm, D_MODEL), lambda i, j: (i, 0)),
                pl.BlockSpec((1, D_MODEL), lambda i, j: (0, 0)),
                _mod_spec(layer, 1, tm),
                _mod_spec(layer, 0, tm),
                pl.BlockSpec((None, D_MODEL, tn), lambda i, j: (widx, 0, j))]
    out_specs = pl.BlockSpec((tm, tn), lambda i, j: (i, j))
    out_shape = jax.ShapeDtypeStruct((TOKENS, n_main), F32)
    args = [x, g, mod, mod, w]
    body = _inproj_kernel
    if w_low is not None:
        nl = w_low.shape[1]
        in_specs.append(pl.BlockSpec((D_MODEL, nl), lambda i, j: (0, 0)))
        out_specs = [out_specs, pl.BlockSpec((tm, nl), lambda i, j: (i, 0))]
        out_shape = [out_shape, jax.ShapeDtypeStruct((TOKENS, nl), F32)]
        args.append(w_low)
        body = _inproj_lowrank_kernel
    return pl.pallas_call(
        body, grid=grid, in_specs=in_specs, out_specs=out_specs, out_shape=out_shape,
        scratch_shapes=[pltpu.VMEM((tm, D_MODEL), BF16)],
        compiler_params=_params("parallel", "arbitrary"),
        name="inproj",
    )(*args)


def _outproj_kernel(a_ref, w_ref, x_ref, gate_ref, o_ref):
    acc = jnp.dot(a_ref[...], w_ref[...].astype(BF16), preferred_element_type=F32)
    o_ref[...] = x_ref[...] + gate_ref[0] * acc


def _outproj(a, w, widx, x, mod, layer, which, tm, tn):
    k = a.shape[1]
    return pl.pallas_call(
        _outproj_kernel, grid=(TOKENS // tm, D_MODEL // tn),
        in_specs=[pl.BlockSpec((tm, k), lambda i, j: (i, 0)),
                  pl.BlockSpec((None, k, tn), lambda i, j: (widx, 0, j)),
                  pl.BlockSpec((tm, tn), lambda i, j: (i, j)),
                  _mod_spec(layer, which, tm, tn)],
        out_specs=pl.BlockSpec((tm, tn), lambda i, j: (i, j)),
        out_shape=jax.ShapeDtypeStruct((TOKENS, D_MODEL), F32),
        input_output_aliases={2: 0},
        compiler_params=_params("parallel", "arbitrary"),
        name="outproj",
    )(a, w, x, mod)


def _ffn_up_kernel(x_ref, g_ref, sc_ref, sh_ref, wg_ref, wu_ref, o_ref, h_scr):
    @pl.when(pl.program_id(1) == 0)
    def _():
        h_scr[...] = _norm_mod(x_ref[...], g_ref[...], sc_ref[0], sh_ref[0])

    h = h_scr[...]
    gt = jnp.dot(h, wg_ref[...].astype(BF16), preferred_element_type=F32)
    up = jnp.dot(h, wu_ref[...].astype(BF16), preferred_element_type=F32)
    o_ref[...] = (_silu(gt) * up).astype(BF16)


def _ffn_up(x, g, mod, layer, w_in):
    tm, tf = 1024, 512
    nf = D_FF // tf
    return pl.pallas_call(
        _ffn_up_kernel, grid=(TOKENS // tm, nf),
        in_specs=[pl.BlockSpec((tm, D_MODEL), lambda i, j: (i, 0)),
                  pl.BlockSpec((1, D_MODEL), lambda i, j: (0, 0)),
                  _mod_spec(layer, 4, tm),
                  _mod_spec(layer, 3, tm),
                  pl.BlockSpec((None, D_MODEL, tf), lambda i, j: (layer, 0, j)),
                  pl.BlockSpec((None, D_MODEL, tf), lambda i, j: (layer, 0, nf + j))],
        out_specs=pl.BlockSpec((tm, tf), lambda i, j: (i, j)),
        out_shape=jax.ShapeDtypeStruct((TOKENS, D_FF), BF16),
        scratch_shapes=[pltpu.VMEM((tm, D_MODEL), BF16)],
        compiler_params=_params("parallel", "arbitrary"),
        name="ffn_up",
    )(x, g, mod, mod, w_in, w_in)


def _final_norm_kernel(x_ref, g_ref, o_ref):
    x = x_ref[...]
    ms = jnp.mean(x * x, axis=-1, keepdims=True)
    o_ref[...] = (x * lax.rsqrt(ms + EPS)) * g_ref[...]


def _final_norm(x, g, row0, rows):
    tm = 512
    t0 = row0 // tm
    return pl.pallas_call(
        _final_norm_kernel, grid=(rows // tm,),
        in_specs=[pl.BlockSpec((tm, D_MODEL), lambda i: (t0 + i, 0)),
                  pl.BlockSpec((1, D_MODEL), lambda i: (0, 0))],
        out_specs=pl.BlockSpec((tm, D_MODEL), lambda i: (i, 0)),
        out_shape=jax.ShapeDtypeStruct((rows, D_MODEL), F32),
        compiler_params=_params("parallel"),
        name="final_norm",
    )(x, g)


def _chunk_cumsum(g, sub, reverse):
    rows, lanes = g.shape
    tiles = rows // SUBLANES
    x = g.reshape(tiles, SUBLANES, lanes)
    s = 1
    while s < SUBLANES:
        if reverse:
            x = x + jnp.where(sub < SUBLANES - s, pltpu.roll(x, SUBLANES - s, 1), 0.0)
        else:
            x = x + jnp.where(sub >= s, pltpu.roll(x, s, 1), 0.0)
        s *= 2
    parts = [None] * tiles
    carry = None
    for j in (range(tiles - 1, -1, -1) if reverse else range(tiles)):
        xj = x[j] if carry is None else x[j] + carry
        parts[j] = xj
        carry = xj[0:1, :] if reverse else xj[SUBLANES - 1:SUBLANES, :]
    return jnp.concatenate(parts, axis=0)


def _gla_kernel(*refs, variant, layer, n_seq, seq_len, hb, dv, has_s0):
    refs = list(refs)
    if variant == "hgrn":
        q_ref, ff_ref, fb_ref, v_ref, gate_ref, lb_ref, gain_ref = refs[:7]
        refs = refs[7:]
        gate_in = (ff_ref, fb_ref)
    else:
        q_ref, k_ref, v_ref, gate_ref, lr_ref, wa_ref, ba_ref, gain_ref = refs[:8]
        refs = refs[8:]
    s0_ref = refs.pop(0) if has_s0 else None
    refs.pop(0)
    out_ref, st_ref = refs[:2]
    o_scr, s_scr = refs[2:4]
    g_scr = refs[4:]

    t = seq_len
    n = t // SCAN_CHUNK
    half = SCAN_CHUNK // 2
    sub = lax.broadcasted_iota(jnp.int32, (SCAN_CHUNK // SUBLANES, SUBLANES, A_DK), 1)
    ri = lax.broadcasted_iota(jnp.int32, (SCAN_CHUNK, SCAN_CHUNK), 0)
    ci = lax.broadcasted_iota(jnp.int32, (SCAN_CHUNK, SCAN_CHUNK), 1)
    masks = (ci <= ri, ci >= ri)
    nt_dims = (((1,), (1,)), ((), ()))
    tn_dims = (((0,), (0,)), ((), ()))
    gain = gain_ref[...]

    rows_total = n_seq * t
    if variant == "hgrn":
        lbr = lb_ref[...]
        ex = jnp.exp(lbr - jnp.max(lbr, axis=0, keepdims=True))
        lb = jnp.sum(ex[:layer + 1], axis=0, keepdims=True) / jnp.sum(ex, axis=0, keepdims=True)
    else:
        def decay_tile(i, carry):
            rows = pl.ds(pl.multiple_of(i * TIME_TILE, TIME_TILE), TIME_TILE)
            lr = lr_ref[rows, :].astype(BF16)
            for d in range(2):
                z = jnp.dot(lr, wa_ref[d].astype(BF16), preferred_element_type=F32) + ba_ref[d:d + 1, :]
                g_scr[d][rows, :] = -_softplus(-z) / GLA_GATE_NORM
            return carry

        lax.fori_loop(0, rows_total // TIME_TILE, decay_tile, 0)

    def chunk_step(c, s, h, d):
        rows = pl.ds(pl.multiple_of(s * t + c * SCAN_CHUNK, SCAN_CHUNK), SCAN_CHUNK)
        kc = slice(h * A_DK, (h + 1) * A_DK)
        vc = slice(h * dv, (h + 1) * dv)
        chain = (s * hb + h) * 2 + d
        q = q_ref[rows, kc] * (A_DK ** -0.5)
        if variant == "hgrn":
            lbh = lb[:, kc]
            f = lbh + (1.0 - lbh) * _sigmoid(gate_in[d][rows, kc])
            k = 1.0 - f
            g = jnp.log(f)
        else:
            k = k_ref[rows, kc]
            g = g_scr[d][rows, kc]
        b = _chunk_cumsum(g, sub, d == 1)
        tot = b[0:1, :] if d == 1 else b[SCAN_CHUNK - 1:SCAN_CHUNK, :]
        mid = b[half:half + 1, :] if d == 1 else b[half - 1:half, :]
        qs = (q * jnp.exp(b)).astype(BF16)
        qa = (q * jnp.exp(b - mid)).astype(BF16)
        ka = (k * jnp.exp(mid - b)).astype(BF16)
        kt = (k * jnp.exp(tot - b)).astype(BF16)
        vb = v_ref[rows, vc].astype(BF16)
        s_t = s_scr[chain]
        att = lax.dot_general(qa, ka, nt_dims, preferred_element_type=F32)
        att = jnp.where(masks[d], att, 0.0).astype(BF16)
        o = jnp.dot(jnp.concatenate([qs, att], axis=1),
                    jnp.concatenate([s_t.astype(BF16), vb], axis=0), preferred_element_type=F32)
        decay = jnp.broadcast_to(jnp.exp(tot), (A_DK, A_DK)).T
        if dv != A_DK:
            decay = jnp.concatenate([decay] * (dv // A_DK), axis=1)
        s_scr[chain] = decay * s_t + lax.dot_general(kt, vb, tn_dims, preferred_element_type=F32)
        return rows, vc, o

    def finish(rows, vc, o):
        o = o + o_scr[rows, vc]
        ms = jnp.mean(o * o, axis=-1, keepdims=True)
        o = (o * lax.rsqrt(ms + EPS)) * gain
        out_ref[rows, vc] = (o * _silu(gate_ref[rows, vc])).astype(out_ref.dtype)

    def first_half(i, carry):
        for s in range(n_seq):
            for h in range(hb):
                for d in range(2):
                    rows, vc, o = chunk_step(i if d == 0 else n - 1 - i, s, h, d)
                    o_scr[rows, vc] = o
        return carry

    def second_half(i, carry):
        for s in range(n_seq):
            for h in range(hb):
                for d in range(2):
                    finish(*chunk_step(i if d == 0 else n - 1 - i, s, h, d))
        return carry

    for s in range(n_seq):
        for h in range(hb):
            for d in range(2):
                chain = (s * hb + h) * 2 + d
                s_scr[chain] = s0_ref[s, 0, d, h] if has_s0 else jnp.zeros((A_DK, dv), F32)
    lax.fori_loop(0, n // 2, first_half, 0)
    lax.fori_loop(n // 2, n, second_half, 0)
    for s in range(n_seq):
        for h in range(hb):
            for d in range(2):
                st_ref[s, 0, d, h] = s_scr[(s * hb + h) * 2 + d]


def _gla_scan(variant, layer, e, proj, lowrank, prev, s0, n_seq, seq_len, row0, small, seq_blk, hb):
    t = seq_len
    rows = seq_blk * t
    tb0 = row0 // rows
    heads, dv = (A_HEADS, A_DV) if variant == "hgrn" else (B_HEADS, B_DV)

    def col_spec(width, col0):
        blk0 = col0 // (hb * width)
        return pl.BlockSpec((rows, hb * width), lambda s, h: (tb0 + s, blk0 + h))

    if variant == "hgrn":
        o_q, o_ff, o_fb, o_v, o_g = 0, A_KDIM, 2 * A_KDIM, 3 * A_KDIM, 3 * A_KDIM + A_VDIM
        in_specs = [col_spec(A_DK, o_q), col_spec(A_DK, o_ff), col_spec(A_DK, o_fb),
                    col_spec(dv, o_v), col_spec(dv, o_g),
                    pl.BlockSpec((DEPTH + 1, hb * A_DK), lambda s, h: (0, h)),
                    pl.BlockSpec((1, dv), lambda s, h: (0, 0))]
        args = [proj] * 5 + [small["lb"], small["gain"]]
        out_col0 = 0
        decay_scratch = []
    else:
        base = 3 * A_KDIM + 2 * A_VDIM
        o_q, o_k, o_v, o_g = base, base + B_KDIM, base + 2 * B_KDIM, base + 2 * B_KDIM + B_VDIM
        in_specs = [col_spec(B_DK, o_q), col_spec(B_DK, o_k),
                    col_spec(dv, o_v), col_spec(dv, o_g),
                    pl.BlockSpec((rows, 2 * GLA_RANK), lambda s, h: (tb0 + s, 0)),
                    pl.BlockSpec((2, 2 * GLA_RANK, hb * B_DK), lambda s, h: (0, 0, h)),
                    pl.BlockSpec((2, hb * B_DK), lambda s, h: (0, h)),
                    pl.BlockSpec((1, dv), lambda s, h: (0, 0))]
        args = [proj] * 4 + [lowrank, small["wa"], small["ba"], small["gain"]]
        out_col0 = A_VDIM
        decay_scratch = [pltpu.VMEM((rows, hb * B_DK), F32)] * 2
    has_s0 = s0 is not None
    if has_s0:
        in_specs.append(pl.BlockSpec((seq_blk, 1, 2, hb, A_DK, dv), lambda s, h: (s, e, 0, h, 0, 0)))
        args.append(s0)
    in_specs.append(pl.BlockSpec(memory_space=pl.ANY))
    aliases = {len(args): 0}
    args.append(prev)
    out_blk0 = out_col0 // (hb * dv)
    merged, states = pl.pallas_call(
        functools.partial(_gla_kernel, variant=variant, layer=layer, n_seq=seq_blk, seq_len=t, hb=hb, dv=dv,
                          has_s0=has_s0),
        grid=(n_seq // seq_blk, heads // hb),
        in_specs=in_specs,
        out_specs=[pl.BlockSpec((rows, hb * dv), lambda s, h: (tb0 + s, out_blk0 + h)),
                   pl.BlockSpec((seq_blk, 1, 2, hb, A_DK, dv), lambda s, h: (s, 0, 0, h, 0, 0))],
        out_shape=[jax.ShapeDtypeStruct((TOKENS, A_VDIM + B_VDIM), BF16),
                   jax.ShapeDtypeStruct((n_seq, 1, 2, heads, A_DK, dv), F32)],
        scratch_shapes=[pltpu.VMEM((rows, hb * dv), F32), pltpu.VMEM((seq_blk * hb * 2, A_DK, dv), F32)]
                       + decay_scratch,
        input_output_aliases=aliases,
        compiler_params=_params("parallel", "arbitrary"),
        name=f"{variant}_scan",
    )(*args)
    return merged, states


def _lru_kernel(*refs, seq_len, row_len, has_s0):
    refs = list(refs)
    xr_ref, gate_ref, cw_ref, cb_ref, rgw_ref, rgb_ref, igw_ref, igb_ref, lam_ref = refs[:9]
    refs = refs[9:]
    s0_ref = refs.pop(0) if has_s0 else None
    refs.pop(0)
    out_ref, st_ref = refs[:2]
    xc_scr, y_scr, a_scr, u_scr, h_scr = refs[2:]

    t = seq_len
    w = LRU_BW
    tl = TIME_TILE
    n_tiles = t // tl
    n_groups = tl // SUBLANES
    pos = lax.broadcasted_iota(jnp.int32, (tl, w), 0) % row_len
    sub = lax.broadcasted_iota(jnp.int32, (n_groups, SUBLANES, w), 1)
    cw = cw_ref[...]

    def conv(x):
        xc = cb_ref[...] + jnp.where(pos >= 2, pltpu.roll(x, 2, 0), 0.0) * cw[0:1, :]
        xc = xc + jnp.where(pos >= 1, pltpu.roll(x, 1, 0), 0.0) * cw[1:2, :]
        xc = xc + x * cw[2:3, :]
        return xc + jnp.where(pos < row_len - 1, pltpu.roll(x, tl - 1, 0), 0.0) * cw[3:4, :]

    def tile_pass(d):
        sp = _softplus(-lam_ref[d:d + 1, :])
        edge = SUBLANES - 1 if d == 0 else 0

        def tile_body(i, hprev):
            tile = i if d == 0 else n_tiles - 1 - i
            r0 = pl.multiple_of(tile * tl, tl)
            rows = pl.ds(r0, tl)
            if d == 0:
                xc = conv(xr_ref[rows, :])
                xc_scr[rows, :] = xc
            else:
                xc = xc_scr[rows, :]
            xcb = xc.astype(BF16)
            r = _sigmoid(jnp.dot(xcb, rgw_ref[d, 0].astype(BF16), preferred_element_type=F32)
                         + rgb_ref[d:d + 1, :])
            ig = _sigmoid(jnp.dot(xcb, igw_ref[d, 0].astype(BF16), preferred_element_type=F32)
                          + igb_ref[d:d + 1, :])
            log_a = (-LRU_C * r) * sp
            a = jnp.exp(log_a)
            u = jnp.sqrt(-jnp.tanh(log_a) * (a * a + 1.0)) * (ig * xc)
            a = a.reshape(n_groups, SUBLANES, w)
            u = u.reshape(n_groups, SUBLANES, w)
            s = 1
            while s < SUBLANES:
                if d == 0:
                    ok = sub >= s
                    u = u + a * jnp.where(ok, pltpu.roll(u, s, 1), 0.0)
                    a = a * jnp.where(ok, pltpu.roll(a, s, 1), 1.0)
                else:
                    ok = sub < SUBLANES - s
                    u = u + a * jnp.where(ok, pltpu.roll(u, SUBLANES - s, 1), 0.0)
                    a = a * jnp.where(ok, pltpu.roll(a, SUBLANES - s, 1), 1.0)
                s *= 2
            a_scr[...] = a.reshape(tl, w)
            u_scr[...] = u.reshape(tl, w)

            def group_body(gi, hp):
                grp = gi if d == 0 else n_groups - 1 - gi
                grows = pl.ds(pl.multiple_of(grp * SUBLANES, SUBLANES), SUBLANES)
                hb = u_scr[grows, :] + a_scr[grows, :] * hp
                h_scr[grows, :] = hb
                return hb[edge:edge + 1, :]

            hlast = lax.fori_loop(0, n_groups, group_body, hprev)
            if d == 0:
                y_scr[rows, :] = h_scr[...]
            else:
                y = y_scr[rows, :] + h_scr[...]
                out_ref[rows, :] = (y * _gelu_tanh(gate_ref[rows, :])).astype(out_ref.dtype)
            return hlast

        h0 = s0_ref[0, 0, d:d + 1, :] if has_s0 else jnp.zeros((1, w), F32)
        st_ref[0, 0, d:d + 1, :] = lax.fori_loop(0, n_tiles, tile_body, h0)

    tile_pass(0)
    tile_pass(1)


def _lru_scan(e, proj, prev, s0, n_seq, seq_len, row_len, row0, p):
    t = seq_len
    tb0 = row0 // t
    has_s0 = s0 is not None
    in_specs = [pl.BlockSpec((t, LRU_BW), lambda s, n: (tb0 + s, n)),
                pl.BlockSpec((t, LRU_BW), lambda s, n: (tb0 + s, LRU_BLOCKS + n)),
                pl.BlockSpec((CONV_W, LRU_BW), lambda s, n: (0, n)),
                pl.BlockSpec((1, LRU_BW), lambda s, n: (0, n)),
                pl.BlockSpec((2, 1, LRU_BW, LRU_BW), lambda s, n: (0, n, 0, 0)),
                pl.BlockSpec((2, LRU_BW), lambda s, n: (0, n)),
                pl.BlockSpec((2, 1, LRU_BW, LRU_BW), lambda s, n: (0, n, 0, 0)),
                pl.BlockSpec((2, LRU_BW), lambda s, n: (0, n)),
                pl.BlockSpec((2, LRU_BW), lambda s, n: (0, n))]
    args = [proj, proj, p["conv_w"], p["conv_b"], p["rg_w"], p["rg_b"], p["ig_w"], p["ig_b"], p["lam"]]
    if has_s0:
        in_specs.append(pl.BlockSpec((1, 1, 2, LRU_BW), lambda s, n: (s, e, 0, n)))
        args.append(s0)
    in_specs.append(pl.BlockSpec(memory_space=pl.ANY))
    aliases = {len(args): 0}
    args.append(prev)
    return pl.pallas_call(
        functools.partial(_lru_kernel, seq_len=t, row_len=row_len, has_s0=has_s0),
        grid=(n_seq, LRU_BLOCKS),
        in_specs=in_specs,
        out_specs=[pl.BlockSpec((t, LRU_BW), lambda s, n: (tb0 + s, n)),
                   pl.BlockSpec((1, 1, 2, LRU_BW), lambda s, n: (s, 0, 0, n))],
        out_shape=[jax.ShapeDtypeStruct((TOKENS, LRU_WIDTH), BF16),
                   jax.ShapeDtypeStruct((n_seq, 1, 2, LRU_WIDTH), F32)],
        scratch_shapes=[pltpu.VMEM((t, LRU_BW), F32), pltpu.VMEM((t, LRU_BW), F32),
                        pltpu.VMEM((TIME_TILE, LRU_BW), F32), pltpu.VMEM((TIME_TILE, LRU_BW), F32),
                        pltpu.VMEM((TIME_TILE, LRU_BW), F32)],
        input_output_aliases=aliases,
        compiler_params=_params("parallel", "arbitrary"),
        name="lru_scan",
    )(*args)


def kernel(x_prompt, x_sample, state_hgrn, state_gla, state_rglru, c, c_ctx, norm1_g, norm2_g, w_mod, b_mod,
           ffn_w_in, ffn_w_out, hgrn_lower_bounds, even_w_in, gla_w_alpha, gla_b_alpha, hgrn_norm_g,
           gla_norm_g, even_w_out, odd_w_in, conv_w, conv_b, rg_w, rg_b, ig_w, ig_b, lru_lambda, odd_w_out,
           final_norm_g):
    x = _stack_tokens(x_prompt.reshape(N_PROMPT, D_MODEL), x_sample.reshape(N_SAMPLE, D_MODEL))
    c3 = jnp.concatenate([c_ctx[None, :], c, jnp.zeros((MOD_ROWS - 1 - DEC_BATCH, D_MODEL), F32)], axis=0)
    mod = _modulation(c3, w_mod, b_mod)

    new_hgrn, new_gla, new_lru = [], [], []
    for l in range(DEPTH):
        e = l // 2
        g1 = norm1_g[l][None, :]
        if l % 2 == 0:
            proj, lowrank = _inproj(x, g1, mod, l, even_w_in, e, EVEN_MAIN, even_w_in[e][:, EVEN_MAIN:])
            wa = jnp.zeros((2, 2 * GLA_RANK, B_KDIM), F32)
            wa = wa.at[0, :GLA_RANK].set(gla_w_alpha[e, 0]).at[1, GLA_RANK:].set(gla_w_alpha[e, 1])
            small_a = {"lb": hgrn_lower_bounds, "gain": hgrn_norm_g[e][None, :]}
            small_b = {"wa": wa, "ba": gla_b_alpha[e], "gain": gla_norm_g[e][None, :]}
            merged = jnp.zeros((TOKENS, A_VDIM + B_VDIM), BF16)
            merged, sa = _gla_scan("hgrn", l, e, proj, None, merged, None, BATCH, SEQ, 0, small_a, 4, 1)
            merged, sb = _gla_scan("gla", l, e, proj, lowrank, merged, None, BATCH, SEQ, 0, small_b, 4, 1)
            merged, _ = _gla_scan("hgrn", l, e, proj, None, merged, state_hgrn, DEC_BATCH, DEC_SEQ, N_PROMPT,
                                  small_a, 2, 2)
            merged, _ = _gla_scan("gla", l, e, proj, lowrank, merged, state_gla, DEC_BATCH, DEC_SEQ, N_PROMPT,
                                  small_b, 2, 1)
            new_hgrn.append(sa)
            new_gla.append(sb)
            w_out = even_w_out
        else:
            proj = _inproj(x, g1, mod, l, odd_w_in, e, 2 * LRU_WIDTH)
            p = {"conv_w": conv_w[e], "conv_b": conv_b[e][None, :], "rg_w": rg_w[e], "rg_b": rg_b[e],
                 "ig_w": ig_w[e], "ig_b": ig_b[e], "lam": lru_lambda[e]}
            merged = jnp.zeros((TOKENS, LRU_WIDTH), BF16)
            merged, sr = _lru_scan(e, proj, merged, None, BATCH, SEQ, SEQ, 0, p)
            merged, _ = _lru_scan(e, proj, merged, state_rglru, DEC_BATCH, DEC_SEQ, GRID_W, N_PROMPT, p)
            new_lru.append(sr)
            w_out = odd_w_out
        x = _outproj(merged, w_out, e, x, mod, l, 2, 1024, 1024)
        hidden = _ffn_up(x, norm2_g[l][None, :], mod, l, ffn_w_in)
        x = _outproj(hidden, ffn_w_out, l, x, mod, l, 5, 1024, 256)

    gf = final_norm_g[None, :]
    y_prompt = _final_norm(x, gf, 0, N_PROMPT)
    y_sample = _final_norm(x, gf, N_PROMPT, N_SAMPLE)
    dt = x_prompt.dtype
    return (y_prompt.reshape(BATCH, SEQ, D_MODEL), y_sample.reshape(DEC_BATCH, DEC_SEQ, D_MODEL),
            jnp.concatenate(new_hgrn, axis=1).astype(dt), jnp.concatenate(new_gla, axis=1).astype(dt),
            jnp.concatenate(new_lru, axis=1).astype(dt))
```

```python
import functools

import jax
import jax.numpy as jnp
from jax import lax
from jax.experimental import pallas as pl
from jax.experimental.pallas import tpu as pltpu

F32 = jnp.float32
BF16 = jnp.bfloat16

D_MODEL = 2048
BATCH = 16
SEQ = 256
DEPTH = 2
DEC_BATCH = 2
DEC_SEQ = 2048
GRID_W = 64
N_EVEN = (DEPTH + 1) // 2
N_ODD = DEPTH // 2
A_HEADS = 8
A_DK = 128
A_DV = 128
A_KDIM = A_HEADS * A_DK
A_VDIM = A_HEADS * A_DV
B_HEADS = 4
B_DK = 128
B_DV = 256
B_KDIM = B_HEADS * B_DK
B_VDIM = B_HEADS * B_DV
GLA_RANK = 16
GLA_GATE_NORM = 16.0
SCAN_CHUNK = 64
EVEN_MAIN = 3 * A_KDIM + 2 * A_VDIM + 2 * B_KDIM + 2 * B_VDIM
LRU_WIDTH = D_MODEL
LRU_BLOCKS = 8
LRU_BW = LRU_WIDTH // LRU_BLOCKS
LRU_C = 8.0
CONV_W = 4
D_FF = (8 * D_MODEL + 3 * 256 - 1) // (3 * 256) * 256
EPS = 1e-6

N_PROMPT = BATCH * SEQ
N_SAMPLE = DEC_BATCH * DEC_SEQ
TOKENS = N_PROMPT + N_SAMPLE
MOD_ROWS = 8
SUBLANES = 8
LANES = 128
TIME_TILE = 256
VMEM_LIMIT = 56 * 1024 * 1024


def _sigmoid(x):
    return 0.5 * (jnp.tanh(0.5 * x) + 1.0)


def _silu(x):
    return x * _sigmoid(x)


def _softplus(x):
    return jnp.maximum(x, 0.0) + jnp.log1p(jnp.exp(-jnp.abs(x)))


def _gelu_tanh(x):
    c = 0.7978845608028654
    return x * (0.5 * (1.0 + jnp.tanh(c * (x + 0.044715 * (x * x * x)))))


def _mod_row(tile, tile_rows):
    r0 = tile * tile_rows
    return jnp.where(r0 < N_PROMPT, 0, 1 + (r0 - N_PROMPT) // DEC_SEQ)


def _mod_spec(layer, which, tile_rows, tile0=0, width=D_MODEL):
    base = (layer * 6 + which) * MOD_ROWS
    if width == D_MODEL:
        return pl.BlockSpec((1, 1, width), lambda *ids: (base + _mod_row(tile0 + ids[0], tile_rows), 0, 0))
    return pl.BlockSpec((1, 1, width), lambda i, j: (base + _mod_row(tile0 + i, tile_rows), 0, j))


def _row_spec():
    return pl.BlockSpec((1, D_MODEL), lambda *ids: (0, 0))


def _norm_mod(x, g, sc, sh):
    ms = jnp.mean(x * x, axis=-1, keepdims=True)
    y = (x * lax.rsqrt(ms + EPS)) * g
    return (y * (1.0 + sc) + sh).astype(BF16)


def _params(*sem):
    return pltpu.CompilerParams(dimension_semantics=sem, vmem_limit_bytes=VMEM_LIMIT)


def _stack_kernel(p_ref, s_ref, g_ref, sc_ref, sh_ref, o_ref, h_ref, *, prompt_tiles):
    def emit(x):
        o_ref[...] = x
        h_ref[...] = _norm_mod(x, g_ref[...], sc_ref[0], sh_ref[0])

    @pl.when(pl.program_id(0) < prompt_tiles)
    def _():
        emit(p_ref[...])

    @pl.when(pl.program_id(0) >= prompt_tiles)
    def _():
        emit(s_ref[...])


def _stack_tokens(xp, xs, g, mod):
    tm = 512
    pt = N_PROMPT // tm
    return pl.pallas_call(
        functools.partial(_stack_kernel, prompt_tiles=pt), grid=(TOKENS // tm,),
        in_specs=[pl.BlockSpec((tm, D_MODEL), lambda i: (jnp.minimum(i, pt - 1), 0)),
                  pl.BlockSpec((tm, D_MODEL), lambda i: (jnp.maximum(i - pt, 0), 0)),
                  _row_spec(), _mod_spec(0, 1, tm), _mod_spec(0, 0, tm)],
        out_specs=[pl.BlockSpec((tm, D_MODEL), lambda i: (i, 0)),
                   pl.BlockSpec((tm, D_MODEL), lambda i: (i, 0))],
        out_shape=[jax.ShapeDtypeStruct((TOKENS, D_MODEL), F32),
                   jax.ShapeDtypeStruct((TOKENS, D_MODEL), BF16)],
        compiler_params=_params("arbitrary"),
        name="stack_tokens",
    )(xp, xs, g, mod, mod)


def _mod_kernel(c_ref, w_ref, b_ref, o_ref):
    s = _silu(c_ref[...]).astype(BF16)
    o_ref[0, 0] = jnp.dot(s, w_ref[0].astype(BF16), preferred_element_type=F32) + b_ref[0]


def _modulation(c3, w_mod, b_mod):
    tn = 1024
    nb = D_MODEL // tn
    out = pl.pallas_call(
        _mod_kernel,
        grid=(DEPTH, 6, nb),
        in_specs=[pl.BlockSpec((MOD_ROWS, D_MODEL), lambda l, k, j: (0, 0)),
                  pl.BlockSpec((1, D_MODEL, tn), lambda l, k, j: (l, 0, k * nb + j)),
                  pl.BlockSpec((1, 1, tn), lambda l, k, j: (l, 0, k * nb + j))],
        out_specs=pl.BlockSpec((1, 1, MOD_ROWS, tn), lambda l, k, j: (l, k, 0, j)),
        out_shape=jax.ShapeDtypeStruct((DEPTH, 6, MOD_ROWS, D_MODEL), F32),
        compiler_params=_params("arbitrary", "arbitrary", "arbitrary"),
        name="modulation",
    )(c3, w_mod, b_mod.reshape(DEPTH, 1, 6 * D_MODEL))
    return out.reshape(DEPTH * 6 * MOD_ROWS, 1, D_MODEL)


def _inproj_kernel(h_ref, w_ref, o_ref):
    o_ref[...] = jnp.dot(h_ref[...], w_ref[...].astype(BF16), preferred_element_type=F32)


def _inproj_lowrank_kernel(h_ref, w_ref, wl_ref, o_ref, ol_ref):
    @pl.when(pl.program_id(1) == 0)
    def _():
        ol_ref[...] = jnp.dot(h_ref[...], wl_ref[...].astype(BF16), preferred_element_type=F32)

    o_ref[...] = jnp.dot(h_ref[...], w_ref[...].astype(BF16), preferred_element_type=F32)


def _inproj(h, w, widx, n_main, w_low=None):
    tm, tn = 2048, 512
    in_specs = [pl.BlockSpec((tm, D_MODEL), lambda i, j: (i, 0)),
                pl.BlockSpec((None, D_MODEL, tn), lambda i, j: (widx, 0, j))]
    out_specs = pl.BlockSpec((tm, tn), lambda i, j: (i, j))
    out_shape = jax.ShapeDtypeStruct((TOKENS, n_main), F32)
    args = [h, w]
    body = _inproj_kernel
    if w_low is not None:
        nl = w_low.shape[1]
        in_specs.append(pl.BlockSpec((D_MODEL, nl), lambda i, j: (0, 0)))
        out_specs = [out_specs, pl.BlockSpec((tm, nl), lambda i, j: (i, 0))]
        out_shape = [out_shape, jax.ShapeDtypeStruct((TOKENS, nl), F32)]
        args.append(w_low)
        body = _inproj_lowrank_kernel
    return pl.pallas_call(
        body, grid=(TOKENS // tm, n_main // tn), in_specs=in_specs, out_specs=out_specs, out_shape=out_shape,
        compiler_params=_params("parallel", "arbitrary"),
        name="inproj",
    )(*args)


def _outproj_kernel(a_ref, w_ref, x_ref, gate_ref, g_ref, sc_ref, sh_ref, o_ref, h_ref, w_scr):
    @pl.when(pl.program_id(0) == 0)
    def _():
        w_scr[...] = w_ref[...].astype(BF16)

    acc = jnp.dot(a_ref[...], w_scr[...], preferred_element_type=F32)
    y = x_ref[...] + gate_ref[0] * acc
    o_ref[...] = y
    h_ref[...] = _norm_mod(y, g_ref[...], sc_ref[0], sh_ref[0])


def _outproj(a, w, widx, x, mod, layer, g2):
    tm = 512
    k = a.shape[1]
    return pl.pallas_call(
        _outproj_kernel, grid=(TOKENS // tm,),
        in_specs=[pl.BlockSpec((tm, k), lambda i: (i, 0)),
                  pl.BlockSpec((None, k, D_MODEL), lambda i: (widx, 0, 0), pipeline_mode=pl.Buffered(1)),
                  pl.BlockSpec((tm, D_MODEL), lambda i: (i, 0)),
                  _mod_spec(layer, 2, tm), _row_spec(), _mod_spec(layer, 4, tm), _mod_spec(layer, 3, tm)],
        out_specs=[pl.BlockSpec((tm, D_MODEL), lambda i: (i, 0)),
                   pl.BlockSpec((tm, D_MODEL), lambda i: (i, 0))],
        out_shape=[jax.ShapeDtypeStruct((TOKENS, D_MODEL), F32),
                   jax.ShapeDtypeStruct((TOKENS, D_MODEL), BF16)],
        scratch_shapes=[pltpu.VMEM((k, D_MODEL), BF16)],
        input_output_aliases={2: 0},
        compiler_params=_params("arbitrary"),
        name="outproj",
    )(a, w, x, mod, g2, mod, mod)


def _ffn_up_kernel(h_ref, wg_ref, wu_ref, o_ref):
    h = h_ref[...]
    gt = jnp.dot(h, wg_ref[...].astype(BF16), preferred_element_type=F32)
    up = jnp.dot(h, wu_ref[...].astype(BF16), preferred_element_type=F32)
    o_ref[...] = (_silu(gt) * up).astype(BF16)


def _ffn_up(h, layer, w_in):
    tm, tf = 2048, 512
    nf = D_FF // tf
    return pl.pallas_call(
        _ffn_up_kernel, grid=(TOKENS // tm, nf),
        in_specs=[pl.BlockSpec((tm, D_MODEL), lambda i, j: (i, 0)),
                  pl.BlockSpec((None, D_MODEL, tf), lambda i, j: (layer, 0, j)),
                  pl.BlockSpec((None, D_MODEL, tf), lambda i, j: (layer, 0, nf + j))],
        out_specs=pl.BlockSpec((tm, tf), lambda i, j: (i, j)),
        out_shape=jax.ShapeDtypeStruct((TOKENS, D_FF), BF16),
        compiler_params=_params("parallel", "arbitrary"),
        name="ffn_up",
    )(h, w_in, w_in)


def _ffn_down_kernel(a_ref, w_ref, x_ref, gate_ref, o_ref):
    acc = jnp.dot(a_ref[...], w_ref[...].astype(BF16), preferred_element_type=F32)
    o_ref[...] = x_ref[...] + gate_ref[0] * acc


def _ffn_down(a, w, x, mod, layer):
    tm, tn = 1024, 256
    return pl.pallas_call(
        _ffn_down_kernel, grid=(TOKENS // tm, D_MODEL // tn),
        in_specs=[pl.BlockSpec((tm, D_FF), lambda i, j: (i, 0)),
                  pl.BlockSpec((None, D_FF, tn), lambda i, j: (layer, 0, j)),
                  pl.BlockSpec((tm, tn), lambda i, j: (i, j)),
                  _mod_spec(layer, 5, tm, width=tn)],
        out_specs=pl.BlockSpec((tm, tn), lambda i, j: (i, j)),
        out_shape=jax.ShapeDtypeStruct((TOKENS, D_MODEL), F32),
        input_output_aliases={2: 0},
        compiler_params=_params("parallel", "arbitrary"),
        name="ffn_down",
    )(a, w, x, mod)


def _norm_mod_kernel(x_ref, g_ref, sc_ref, sh_ref, h_ref):
    h_ref[...] = _norm_mod(x_ref[...], g_ref[...], sc_ref[0], sh_ref[0])


def _norm_mod_rows(x, g, mod, layer):
    tm = 512
    return pl.pallas_call(
        _norm_mod_kernel, grid=(TOKENS // tm,),
        in_specs=[pl.BlockSpec((tm, D_MODEL), lambda i: (i, 0)),
                  _row_spec(), _mod_spec(layer, 1, tm), _mod_spec(layer, 0, tm)],
        out_specs=pl.BlockSpec((tm, D_MODEL), lambda i: (i, 0)),
        out_shape=jax.ShapeDtypeStruct((TOKENS, D_MODEL), BF16),
        compiler_params=_params("parallel"),
        name="norm_mod",
    )(x, g, mod, mod)


def _final_norm_kernel(x_ref, g_ref, o_ref):
    x = x_ref[...]
    ms = jnp.mean(x * x, axis=-1, keepdims=True)
    o_ref[...] = (x * lax.rsqrt(ms + EPS)) * g_ref[...]


def _final_norm(x, g, row0, rows):
    tm = 512
    t0 = row0 // tm
    return pl.pallas_call(
        _final_norm_kernel, grid=(rows // tm,),
        in_specs=[pl.BlockSpec((tm, D_MODEL), lambda i: (t0 + i, 0)), _row_spec()],
        out_specs=pl.BlockSpec((tm, D_MODEL), lambda i: (i, 0)),
        out_shape=jax.ShapeDtypeStruct((rows, D_MODEL), F32),
        compiler_params=_params("parallel"),
        name="final_norm",
    )(x, g)


def _chunk_cumsum(g, sub, reverse):
    rows, lanes = g.shape
    tiles = rows // SUBLANES
    x = g.reshape(tiles, SUBLANES, lanes)
    s = 1
    while s < SUBLANES:
        if reverse:
            x = x + jnp.where(sub < SUBLANES - s, pltpu.roll(x, SUBLANES - s, 1), 0.0)
        else:
            x = x + jnp.where(sub >= s, pltpu.roll(x, s, 1), 0.0)
        s *= 2
    parts = [None] * tiles
    carry = None
    for j in (range(tiles - 1, -1, -1) if reverse else range(tiles)):
        xj = x[j] if carry is None else x[j] + carry
        parts[j] = xj
        carry = xj[0:1, :] if reverse else xj[SUBLANES - 1:SUBLANES, :]
    return jnp.concatenate(parts, axis=0)


def _gla_kernel(*refs, variant, layer, n_seq, seq_len, hb, dv, has_s0):
    refs = list(refs)
    if variant == "hgrn":
        q_ref, ff_ref, fb_ref, v_ref, gate_ref, lb_ref, gain_ref = refs[:7]
        refs = refs[7:]
        gate_in = (ff_ref, fb_ref)
    else:
        q_ref, k_ref, v_ref, gate_ref, lr_ref, wa_ref, ba_ref, gain_ref = refs[:8]
        refs = refs[8:]
    s0_ref = refs.pop(0) if has_s0 else None
    refs.pop(0)
    out_ref, st_ref = refs[:2]
    o_scr, s_scr = refs[2:4]
    g_scr = refs[4:]

    t = seq_len
    n = t // SCAN_CHUNK
    half = SCAN_CHUNK // 2
    sub = lax.broadcasted_iota(jnp.int32, (SCAN_CHUNK // SUBLANES, SUBLANES, A_DK), 1)
    ri = lax.broadcasted_iota(jnp.int32, (SCAN_CHUNK, SCAN_CHUNK), 0)
    ci = lax.broadcasted_iota(jnp.int32, (SCAN_CHUNK, SCAN_CHUNK), 1)
    masks = (ci <= ri, ci >= ri)
    nt_dims = (((1,), (1,)), ((), ()))
    tn_dims = (((0,), (0,)), ((), ()))
    gain = gain_ref[...]

    rows_total = n_seq * t
    if variant == "hgrn":
        lbr = lb_ref[...]
        ex = jnp.exp(lbr - jnp.max(lbr, axis=0, keepdims=True))
        lb = jnp.sum(ex[:layer + 1], axis=0, keepdims=True) / jnp.sum(ex, axis=0, keepdims=True)
    else:
        def decay_tile(i, carry):
            rows = pl.ds(pl.multiple_of(i * TIME_TILE, TIME_TILE), TIME_TILE)
            lr = lr_ref[rows, :].astype(BF16)
            for d in range(2):
                z = jnp.dot(lr, wa_ref[d].astype(BF16), preferred_element_type=F32) + ba_ref[d:d + 1, :]
                g_scr[d][rows, :] = -_softplus(-z) / GLA_GATE_NORM
            return carry

        lax.fori_loop(0, rows_total // TIME_TILE, decay_tile, 0)

    def chunk_step(c, s, h, d):
        rows = pl.ds(pl.multiple_of(s * t + c * SCAN_CHUNK, SCAN_CHUNK), SCAN_CHUNK)
        kc = slice(h * A_DK, (h + 1) * A_DK)
        vc = slice(h * dv, (h + 1) * dv)
        chain = (s * hb + h) * 2 + d
        q = q_ref[rows, kc] * (A_DK ** -0.5)
        if variant == "hgrn":
            lbh = lb[:, kc]
            f = lbh + (1.0 - lbh) * _sigmoid(gate_in[d][rows, kc])
            k = 1.0 - f
            g = jnp.log(f)
        else:
            k = k_ref[rows, kc]
            g = g_scr[d][rows, kc]
        b = _chunk_cumsum(g, sub, d == 1)
        tot = b[0:1, :] if d == 1 else b[SCAN_CHUNK - 1:SCAN_CHUNK, :]
        mid = b[half:half + 1, :] if d == 1 else b[half - 1:half, :]
        qs = (q * jnp.exp(b)).astype(BF16)
        qa = (q * jnp.exp(b - mid)).astype(BF16)
        ka = (k * jnp.exp(mid - b)).astype(BF16)
        kt = (k * jnp.exp(tot - b)).astype(BF16)
        vb = v_ref[rows, vc].astype(BF16)
        s_t = s_scr[chain]
        att = lax.dot_general(qa, ka, nt_dims, preferred_element_type=F32)
        att = jnp.where(masks[d], att, 0.0).astype(BF16)
        o = jnp.dot(jnp.concatenate([qs, att], axis=1),
                    jnp.concatenate([s_t.astype(BF16), vb], axis=0), preferred_element_type=F32)
        decay = jnp.broadcast_to(jnp.exp(tot), (A_DK, A_DK)).T
        if dv != A_DK:
            decay = jnp.concatenate([decay] * (dv // A_DK), axis=1)
        s_scr[chain] = decay * s_t + lax.dot_general(kt, vb, tn_dims, preferred_element_type=F32)
        return rows, vc, o

    def finish(rows, vc, o):
        o = o + o_scr[rows, vc]
        ms = jnp.mean(o * o, axis=-1, keepdims=True)
        o = (o * lax.rsqrt(ms + EPS)) * gain
        out_ref[rows, vc] = (o * _silu(gate_ref[rows, vc])).astype(out_ref.dtype)

    def first_half(i, carry):
        for s in range(n_seq):
            for h in range(hb):
                for d in range(2):
                    rows, vc, o = chunk_step(i if d == 0 else n - 1 - i, s, h, d)
                    o_scr[rows, vc] = o
        return carry

    def second_half(i, carry):
        for s in range(n_seq):
            for h in range(hb):
                for d in range(2):
                    finish(*chunk_step(i if d == 0 else n - 1 - i, s, h, d))
        return carry

    for s in range(n_seq):
        for h in range(hb):
            for d in range(2):
                chain = (s * hb + h) * 2 + d
                s_scr[chain] = s0_ref[s, 0, d, h] if has_s0 else jnp.zeros((A_DK, dv), F32)
    lax.fori_loop(0, n // 2, first_half, 0)
    lax.fori_loop(n // 2, n, second_half, 0)
    for s in range(n_seq):
        for h in range(hb):
            for d in range(2):
                st_ref[s, 0, d, h] = s_scr[(s * hb + h) * 2 + d]


def _gla_scan(variant, layer, e, proj, lowrank, prev, s0, n_seq, seq_len, row0, small, seq_blk, hb):
    t = seq_len
    rows = seq_blk * t
    tb0 = row0 // rows
    heads, dv = (A_HEADS, A_DV) if variant == "hgrn" else (B_HEADS, B_DV)

    def col_spec(width, col0):
        blk0 = col0 // (hb * width)
        return pl.BlockSpec((rows, hb * width), lambda s, h: (tb0 + s, blk0 + h))

    if variant == "hgrn":
        o_q, o_ff, o_fb, o_v, o_g = 0, A_KDIM, 2 * A_KDIM, 3 * A_KDIM, 3 * A_KDIM + A_VDIM
        in_specs = [col_spec(A_DK, o_q), col_spec(A_DK, o_ff), col_spec(A_DK, o_fb),
                    col_spec(dv, o_v), col_spec(dv, o_g),
                    pl.BlockSpec((DEPTH + 1, hb * A_DK), lambda s, h: (0, h)),
                    pl.BlockSpec((1, dv), lambda s, h: (0, 0))]
        args = [proj] * 5 + [small["lb"], small["gain"]]
        out_col0 = 0
        decay_scratch = []
    else:
        base = 3 * A_KDIM + 2 * A_VDIM
        o_q, o_k, o_v, o_g = base, base + B_KDIM, base + 2 * B_KDIM, base + 2 * B_KDIM + B_VDIM
        in_specs = [col_spec(B_DK, o_q), col_spec(B_DK, o_k),
                    col_spec(dv, o_v), col_spec(dv, o_g),
                    pl.BlockSpec((rows, 2 * GLA_RANK), lambda s, h: (tb0 + s, 0)),
                    pl.BlockSpec((2, 2 * GLA_RANK, hb * B_DK), lambda s, h: (0, 0, h)),
                    pl.BlockSpec((2, hb * B_DK), lambda s, h: (0, h)),
                    pl.BlockSpec((1, dv), lambda s, h: (0, 0))]
        args = [proj] * 4 + [lowrank, small["wa"], small["ba"], small["gain"]]
        out_col0 = A_VDIM
        decay_scratch = [pltpu.VMEM((rows, hb * B_DK), F32)] * 2
    has_s0 = s0 is not None
    if has_s0:
        in_specs.append(pl.BlockSpec((seq_blk, 1, 2, hb, A_DK, dv), lambda s, h: (s, e, 0, h, 0, 0)))
        args.append(s0)
    in_specs.append(pl.BlockSpec(memory_space=pl.ANY))
    aliases = {len(args): 0}
    args.append(prev)
    out_blk0 = out_col0 // (hb * dv)
    merged, states = pl.pallas_call(
        functools.partial(_gla_kernel, variant=variant, layer=layer, n_seq=seq_blk, seq_len=t, hb=hb, dv=dv,
                          has_s0=has_s0),
        grid=(n_seq // seq_blk, heads // hb),
        in_specs=in_specs,
        out_specs=[pl.BlockSpec((rows, hb * dv), lambda s, h: (tb0 + s, out_blk0 + h)),
                   pl.BlockSpec((seq_blk, 1, 2, hb, A_DK, dv), lambda s, h: (s, 0, 0, h, 0, 0))],
        out_shape=[jax.ShapeDtypeStruct((TOKENS, A_VDIM + B_VDIM), BF16),
                   jax.ShapeDtypeStruct((n_seq, 1, 2, heads, A_DK, dv), F32)],
        scratch_shapes=[pltpu.VMEM((rows, hb * dv), F32), pltpu.VMEM((seq_blk * hb * 2, A_DK, dv), F32)]
                       + decay_scratch,
        input_output_aliases=aliases,
        compiler_params=_params("parallel", "arbitrary"),
        name=f"{variant}_scan",
    )(*args)
    return merged, states


def _lru_kernel(*refs, seq_len, row_len, has_s0):
    refs = list(refs)
    xr_ref, gate_ref, cw_ref, cb_ref, rgw_ref, rgb_ref, igw_ref, igb_ref, lam_ref = refs[:9]
    refs = refs[9:]
    s0_ref = refs.pop(0) if has_s0 else None
    refs.pop(0)
    out_ref, st_ref = refs[:2]
    xc_scr, y_scr, a_scr, u_scr, h_scr = refs[2:]

    t = seq_len
    w = LRU_BW
    tl = TIME_TILE
    n_tiles = t // tl
    n_groups = tl // SUBLANES
    pos = lax.broadcasted_iota(jnp.int32, (tl, w), 0) % row_len
    sub = lax.broadcasted_iota(jnp.int32, (n_groups, SUBLANES, w), 1)
    cw = cw_ref[...]

    def conv(x):
        xc = cb_ref[...] + jnp.where(pos >= 2, pltpu.roll(x, 2, 0), 0.0) * cw[0:1, :]
        xc = xc + jnp.where(pos >= 1, pltpu.roll(x, 1, 0), 0.0) * cw[1:2, :]
        xc = xc + x * cw[2:3, :]
        return xc + jnp.where(pos < row_len - 1, pltpu.roll(x, tl - 1, 0), 0.0) * cw[3:4, :]

    def tile_pass(d):
        sp = _softplus(-lam_ref[d:d + 1, :])
        edge = SUBLANES - 1 if d == 0 else 0

        def tile_body(i, hprev):
            tile = i if d == 0 else n_tiles - 1 - i
            r0 = pl.multiple_of(tile * tl, tl)
            rows = pl.ds(r0, tl)
            if d == 0:
                xc = conv(xr_ref[rows, :])
                xc_scr[rows, :] = xc
            else:
                xc = xc_scr[rows, :]
            xcb = xc.astype(BF16)
            r = _sigmoid(jnp.dot(xcb, rgw_ref[d, 0].astype(BF16), preferred_element_type=F32)
                         + rgb_ref[d:d + 1, :])
            ig = _sigmoid(jnp.dot(xcb, igw_ref[d, 0].astype(BF16), preferred_element_type=F32)
                          + igb_ref[d:d + 1, :])
            log_a = (-LRU_C * r) * sp
            a = jnp.exp(log_a)
            u = jnp.sqrt(-jnp.tanh(log_a) * (a * a + 1.0)) * (ig * xc)
            a = a.reshape(n_groups, SUBLANES, w)
            u = u.reshape(n_groups, SUBLANES, w)
            s = 1
            while s < SUBLANES:
                if d == 0:
                    ok = sub >= s
                    u = u + a * jnp.where(ok, pltpu.roll(u, s, 1), 0.0)
                    a = a * jnp.where(ok, pltpu.roll(a, s, 1), 1.0)
                else:
                    ok = sub < SUBLANES - s
                    u = u + a * jnp.where(ok, pltpu.roll(u, SUBLANES - s, 1), 0.0)
                    a = a * jnp.where(ok, pltpu.roll(a, SUBLANES - s, 1), 1.0)
                s *= 2
            a_scr[...] = a.reshape(tl, w)
            u_scr[...] = u.reshape(tl, w)

            def group_body(gi, hp):
                grp = gi if d == 0 else n_groups - 1 - gi
                grows = pl.ds(pl.multiple_of(grp * SUBLANES, SUBLANES), SUBLANES)
                hb = u_scr[grows, :] + a_scr[grows, :] * hp
                h_scr[grows, :] = hb
                return hb[edge:edge + 1, :]

            hlast = lax.fori_loop(0, n_groups, group_body, hprev)
            if d == 0:
                y_scr[rows, :] = h_scr[...]
            else:
                y = y_scr[rows, :] + h_scr[...]
                out_ref[rows, :] = (y * _gelu_tanh(gate_ref[rows, :])).astype(out_ref.dtype)
            return hlast

        h0 = s0_ref[0, 0, d:d + 1, :] if has_s0 else jnp.zeros((1, w), F32)
        st_ref[0, 0, d:d + 1, :] = lax.fori_loop(0, n_tiles, tile_body, h0)

    tile_pass(0)
    tile_pass(1)


def _lru_scan(e, proj, prev, s0, n_seq, seq_len, row_len, row0, p):
    t = seq_len
    tb0 = row0 // t
    has_s0 = s0 is not None
    in_specs = [pl.BlockSpec((t, LRU_BW), lambda s, n: (tb0 + s, n)),
                pl.BlockSpec((t, LRU_BW), lambda s, n: (tb0 + s, LRU_BLOCKS + n)),
                pl.BlockSpec((CONV_W, LRU_BW), lambda s, n: (0, n)),
                pl.BlockSpec((1, LRU_BW), lambda s, n: (0, n)),
                pl.BlockSpec((2, 1, LRU_BW, LRU_BW), lambda s, n: (0, n, 0, 0)),
                pl.BlockSpec((2, LRU_BW), lambda s, n: (0, n)),
                pl.BlockSpec((2, 1, LRU_BW, LRU_BW), lambda s, n: (0, n, 0, 0)),
                pl.BlockSpec((2, LRU_BW), lambda s, n: (0, n)),
                pl.BlockSpec((2, LRU_BW), lambda s, n: (0, n))]
    args = [proj, proj, p["conv_w"], p["conv_b"], p["rg_w"], p["rg_b"], p["ig_w"], p["ig_b"], p["lam"]]
    if has_s0:
        in_specs.append(pl.BlockSpec((1, 1, 2, LRU_BW), lambda s, n: (s, e, 0, n)))
        args.append(s0)
    in_specs.append(pl.BlockSpec(memory_space=pl.ANY))
    aliases = {len(args): 0}
    args.append(prev)
    return pl.pallas_call(
        functools.partial(_lru_kernel, seq_len=t, row_len=row_len, has_s0=has_s0),
        grid=(n_seq, LRU_BLOCKS),
        in_specs=in_specs,
        out_specs=[pl.BlockSpec((t, LRU_BW), lambda s, n: (tb0 + s, n)),
                   pl.BlockSpec((1, 1, 2, LRU_BW), lambda s, n: (s, 0, 0, n))],
        out_shape=[jax.ShapeDtypeStruct((TOKENS, LRU_WIDTH), BF16),
                   jax.ShapeDtypeStruct((n_seq, 1, 2, LRU_WIDTH), F32)],
        scratch_shapes=[pltpu.VMEM((t, LRU_BW), F32), pltpu.VMEM((t, LRU_BW), F32),
                        pltpu.VMEM((TIME_TILE, LRU_BW), F32), pltpu.VMEM((TIME_TILE, LRU_BW), F32),
                        pltpu.VMEM((TIME_TILE, LRU_BW), F32)],
        input_output_aliases=aliases,
        compiler_params=_params("parallel", "arbitrary"),
        name="lru_scan",
    )(*args)


def kernel(x_prompt, x_sample, state_hgrn, state_gla, state_rglru, c, c_ctx, norm1_g, norm2_g, w_mod, b_mod,
           ffn_w_in, ffn_w_out, hgrn_lower_bounds, even_w_in, gla_w_alpha, gla_b_alpha, hgrn_norm_g,
           gla_norm_g, even_w_out, odd_w_in, conv_w, conv_b, rg_w, rg_b, ig_w, ig_b, lru_lambda, odd_w_out,
           final_norm_g):
    c3 = jnp.concatenate([c_ctx[None, :], c, jnp.zeros((MOD_ROWS - 1 - DEC_BATCH, D_MODEL), F32)], axis=0)
    mod = _modulation(c3, w_mod, b_mod)
    x, h = _stack_tokens(x_prompt.reshape(N_PROMPT, D_MODEL), x_sample.reshape(N_SAMPLE, D_MODEL),
                         norm1_g[0][None, :], mod)

    new_hgrn, new_gla, new_lru = [], [], []
    y_prompt = y_sample = None
    for l in range(DEPTH):
        e = l // 2
        if l % 2 == 0:
            proj, lowrank = _inproj(h, even_w_in, e, EVEN_MAIN, even_w_in[e][:, EVEN_MAIN:])
            wa = jnp.zeros((2, 2 * GLA_RANK, B_KDIM), F32)
            wa = wa.at[0, :GLA_RANK].set(gla_w_alpha[e, 0]).at[1, GLA_RANK:].set(gla_w_alpha[e, 1])
            small_a = {"lb": hgrn_lower_bounds, "gain": hgrn_norm_g[e][None, :]}
            small_b = {"wa": wa, "ba": gla_b_alpha[e], "gain": gla_norm_g[e][None, :]}
            merged = jnp.zeros((TOKENS, A_VDIM + B_VDIM), BF16)
            merged, sa = _gla_scan("hgrn", l, e, proj, None, merged, None, BATCH, SEQ, 0, small_a, 4, 1)
            merged, sb = _gla_scan("gla", l, e, proj, lowrank, merged, None, BATCH, SEQ, 0, small_b, 4, 1)
            merged, _ = _gla_scan("hgrn", l, e, proj, None, merged, state_hgrn, DEC_BATCH, DEC_SEQ, N_PROMPT,
                                  small_a, 2, 2)
            merged, _ = _gla_scan("gla", l, e, proj, lowrank, merged, state_gla, DEC_BATCH, DEC_SEQ, N_PROMPT,
                                  small_b, 2, 1)
            new_hgrn.append(sa)
            new_gla.append(sb)
            w_out = even_w_out
        else:
            proj = _inproj(h, odd_w_in, e, 2 * LRU_WIDTH)
            p = {"conv_w": conv_w[e], "conv_b": conv_b[e][None, :], "rg_w": rg_w[e], "rg_b": rg_b[e],
                 "ig_w": ig_w[e], "ig_b": ig_b[e], "lam": lru_lambda[e]}
            merged = jnp.zeros((TOKENS, LRU_WIDTH), BF16)
            merged, sr = _lru_scan(e, proj, merged, None, BATCH, SEQ, SEQ, 0, p)
            merged, _ = _lru_scan(e, proj, merged, state_rglru, DEC_BATCH, DEC_SEQ, GRID_W, N_PROMPT, p)
            new_lru.append(sr)
            w_out = odd_w_out
        x, h = _outproj(merged, w_out, e, x, mod, l, norm2_g[l][None, :])
        hidden = _ffn_up(h, l, ffn_w_in)
        x = _ffn_down(hidden, ffn_w_out, x, mod, l)
        if l + 1 < DEPTH:
            h = _norm_mod_rows(x, norm1_g[l + 1][None, :], mod, l + 1)
    gf = final_norm_g[None, :]
    y_prompt = _final_norm(x, gf, 0, N_PROMPT)
    y_sample = _final_norm(x, gf, N_PROMPT, N_SAMPLE)
    dt = x_prompt.dtype
    return (y_prompt.reshape(BATCH, SEQ, D_MODEL), y_sample.reshape(DEC_BATCH, DEC_SEQ, D_MODEL),
            jnp.concatenate(new_hgrn, axis=1).astype(dt), jnp.concatenate(new_gla, axis=1).astype(dt),
            jnp.concatenate(new_lru, axis=1).astype(dt))
```

```python
import functools

import jax
import jax.numpy as jnp
from jax import lax
from jax.experimental import pallas as pl
from jax.experimental.pallas import tpu as pltpu

F32 = jnp.float32
BF16 = jnp.bfloat16

D_MODEL = 2048
BATCH = 16
SEQ = 256
DEPTH = 2
DEC_BATCH = 2
DEC_SEQ = 2048
GRID_W = 64
N_EVEN = (DEPTH + 1) // 2
N_ODD = DEPTH // 2
A_HEADS = 8
A_DK = 128
A_DV = 128
A_KDIM = A_HEADS * A_DK
A_VDIM = A_HEADS * A_DV
B_HEADS = 4
B_DK = 128
B_DV = 256
B_KDIM = B_HEADS * B_DK
B_VDIM = B_HEADS * B_DV
GLA_RANK = 16
GLA_GATE_NORM = 16.0
SCAN_CHUNK = 64
EVEN_MAIN = 3 * A_KDIM + 2 * A_VDIM + 2 * B_KDIM + 2 * B_VDIM
LRU_WIDTH = D_MODEL
LRU_BLOCKS = 8
LRU_BW = LRU_WIDTH // LRU_BLOCKS
LRU_C = 8.0
CONV_W = 4
D_FF = (8 * D_MODEL + 3 * 256 - 1) // (3 * 256) * 256
EPS = 1e-6

N_PROMPT = BATCH * SEQ
N_SAMPLE = DEC_BATCH * DEC_SEQ
TOKENS = N_PROMPT + N_SAMPLE
MOD_ROWS = 8
SUBLANES = 8
LANES = 128
TIME_TILE = 256
VMEM_LIMIT = 56 * 1024 * 1024


def _sigmoid(x):
    return 0.5 * (jnp.tanh(0.5 * x) + 1.0)


def _silu(x):
    return x * _sigmoid(x)


def _softplus(x):
    return jnp.maximum(x, 0.0) + jnp.log1p(jnp.exp(-jnp.abs(x)))


def _gelu_tanh(x):
    c = 0.7978845608028654
    return x * (0.5 * (1.0 + jnp.tanh(c * (x + 0.044715 * (x * x * x)))))


def _mod_row(tile, tile_rows):
    r0 = tile * tile_rows
    return jnp.where(r0 < N_PROMPT, 0, 1 + (r0 - N_PROMPT) // DEC_SEQ)


def _mod_spec(layer, which, tile_rows, tile0=0, width=D_MODEL):
    base = (layer * 6 + which) * MOD_ROWS
    if width == D_MODEL:
        return pl.BlockSpec((1, 1, width), lambda *ids: (base + _mod_row(tile0 + ids[0], tile_rows), 0, 0))
    return pl.BlockSpec((1, 1, width), lambda i, j: (base + _mod_row(tile0 + i, tile_rows), 0, j))


def _row_spec():
    return pl.BlockSpec((1, D_MODEL), lambda *ids: (0, 0))


def _norm_mod(x, g, sc, sh):
    ms = jnp.mean(x * x, axis=-1, keepdims=True)
    y = (x * lax.rsqrt(ms + EPS)) * g
    return (y * (1.0 + sc) + sh).astype(BF16)


def _params(*sem):
    return pltpu.CompilerParams(dimension_semantics=sem, vmem_limit_bytes=VMEM_LIMIT)


def _stack_kernel(p_ref, s_ref, g_ref, sc_ref, sh_ref, o_ref, h_ref, *, prompt_tiles):
    def emit(x):
        o_ref[...] = x
        h_ref[...] = _norm_mod(x, g_ref[...], sc_ref[0], sh_ref[0])

    @pl.when(pl.program_id(0) < prompt_tiles)
    def _():
        emit(p_ref[...])

    @pl.when(pl.program_id(0) >= prompt_tiles)
    def _():
        emit(s_ref[...])


def _stack_tokens(xp, xs, g, mod):
    tm = 512
    pt = N_PROMPT // tm
    return pl.pallas_call(
        functools.partial(_stack_kernel, prompt_tiles=pt), grid=(TOKENS // tm,),
        in_specs=[pl.BlockSpec((tm, D_MODEL), lambda i: (jnp.minimum(i, pt - 1), 0)),
                  pl.BlockSpec((tm, D_MODEL), lambda i: (jnp.maximum(i - pt, 0), 0)),
                  _row_spec(), _mod_spec(0, 1, tm), _mod_spec(0, 0, tm)],
        out_specs=[pl.BlockSpec((tm, D_MODEL), lambda i: (i, 0)),
                   pl.BlockSpec((tm, D_MODEL), lambda i: (i, 0))],
        out_shape=[jax.ShapeDtypeStruct((TOKENS, D_MODEL), F32),
                   jax.ShapeDtypeStruct((TOKENS, D_MODEL), BF16)],
        compiler_params=_params("arbitrary"),
        name="stack_tokens",
    )(xp, xs, g, mod, mod)


def _mod_kernel(c_ref, w_ref, b_ref, o_ref):
    s = _silu(c_ref[...]).astype(BF16)
    o_ref[0, 0] = jnp.dot(s, w_ref[0].astype(BF16), preferred_element_type=F32) + b_ref[0]


def _modulation(c3, w_mod, b_mod):
    tn = 1024
    nb = D_MODEL // tn
    out = pl.pallas_call(
        _mod_kernel,
        grid=(DEPTH, 6, nb),
        in_specs=[pl.BlockSpec((MOD_ROWS, D_MODEL), lambda l, k, j: (0, 0)),
                  pl.BlockSpec((1, D_MODEL, tn), lambda l, k, j: (l, 0, k * nb + j)),
                  pl.BlockSpec((1, 1, tn), lambda l, k, j: (l, 0, k * nb + j))],
        out_specs=pl.BlockSpec((1, 1, MOD_ROWS, tn), lambda l, k, j: (l, k, 0, j)),
        out_shape=jax.ShapeDtypeStruct((DEPTH, 6, MOD_ROWS, D_MODEL), F32),
        compiler_params=_params("arbitrary", "arbitrary", "arbitrary"),
        name="modulation",
    )(c3, w_mod, b_mod.reshape(DEPTH, 1, 6 * D_MODEL))
    return out.reshape(DEPTH * 6 * MOD_ROWS, 1, D_MODEL)


def _inproj_kernel(h_ref, w_ref, o_ref):
    o_ref[...] = jnp.dot(h_ref[...], w_ref[...].astype(BF16), preferred_element_type=F32)


def _inproj_lowrank_kernel(h_ref, w_ref, wl_ref, o_ref, ol_ref):
    @pl.when(pl.program_id(1) == 0)
    def _():
        ol_ref[...] = jnp.dot(h_ref[...], wl_ref[...].astype(BF16), preferred_element_type=F32)

    o_ref[...] = jnp.dot(h_ref[...], w_ref[...].astype(BF16), preferred_element_type=F32)


def _inproj(h, w, widx, n_main, w_low=None):
    tm, tn = 2048, 512
    in_specs = [pl.BlockSpec((tm, D_MODEL), lambda i, j: (i, 0)),
                pl.BlockSpec((None, D_MODEL, tn), lambda i, j: (widx, 0, j))]
    out_specs = pl.BlockSpec((tm, tn), lambda i, j: (i, j))
    out_shape = jax.ShapeDtypeStruct((TOKENS, n_main), F32)
    args = [h, w]
    body = _inproj_kernel
    if w_low is not None:
        nl = w_low.shape[1]
        in_specs.append(pl.BlockSpec((D_MODEL, nl), lambda i, j: (0, 0)))
        out_specs = [out_specs, pl.BlockSpec((tm, nl), lambda i, j: (i, 0))]
        out_shape = [out_shape, jax.ShapeDtypeStruct((TOKENS, nl), F32)]
        args.append(w_low)
        body = _inproj_lowrank_kernel
    return pl.pallas_call(
        body, grid=(TOKENS // tm, n_main // tn), in_specs=in_specs, out_specs=out_specs, out_shape=out_shape,
        compiler_params=_params("parallel", "arbitrary"),
        name="inproj",
    )(*args)


def _outproj_kernel(a_ref, w_ref, x_ref, gate_ref, g_ref, sc_ref, sh_ref, o_ref, h_ref, w_scr):
    @pl.when(pl.program_id(0) == 0)
    def _():
        w_scr[...] = w_ref[...].astype(BF16)

    acc = jnp.dot(a_ref[...], w_scr[...], preferred_element_type=F32)
    y = x_ref[...] + gate_ref[0] * acc
    o_ref[...] = y
    h_ref[...] = _norm_mod(y, g_ref[...], sc_ref[0], sh_ref[0])


def _outproj(a, w, widx, x, mod, layer, g2):
    tm = 512
    k = a.shape[1]
    return pl.pallas_call(
        _outproj_kernel, grid=(TOKENS // tm,),
        in_specs=[pl.BlockSpec((tm, k), lambda i: (i, 0)),
                  pl.BlockSpec((None, k, D_MODEL), lambda i: (widx, 0, 0), pipeline_mode=pl.Buffered(1)),
                  pl.BlockSpec((tm, D_MODEL), lambda i: (i, 0)),
                  _mod_spec(layer, 2, tm), _row_spec(), _mod_spec(layer, 4, tm), _mod_spec(layer, 3, tm)],
        out_specs=[pl.BlockSpec((tm, D_MODEL), lambda i: (i, 0)),
                   pl.BlockSpec((tm, D_MODEL), lambda i: (i, 0))],
        out_shape=[jax.ShapeDtypeStruct((TOKENS, D_MODEL), F32),
                   jax.ShapeDtypeStruct((TOKENS, D_MODEL), BF16)],
        scratch_shapes=[pltpu.VMEM((k, D_MODEL), BF16)],
        input_output_aliases={2: 0},
        compiler_params=_params("arbitrary"),
        name="outproj",
    )(a, w, x, mod, g2, mod, mod)


def _ffn_up_kernel(h_ref, wg_ref, wu_ref, o_ref):
    h = h_ref[...]
    gt = jnp.dot(h, wg_ref[...].astype(BF16), preferred_element_type=F32)
    up = jnp.dot(h, wu_ref[...].astype(BF16), preferred_element_type=F32)
    o_ref[...] = (_silu(gt) * up).astype(BF16)


def _ffn_up(h, layer, w_in):
    tm, tf = 2048, 512
    nf = D_FF // tf
    return pl.pallas_call(
        _ffn_up_kernel, grid=(TOKENS // tm, nf),
        in_specs=[pl.BlockSpec((tm, D_MODEL), lambda i, j: (i, 0)),
                  pl.BlockSpec((None, D_MODEL, tf), lambda i, j: (layer, 0, j)),
                  pl.BlockSpec((None, D_MODEL, tf), lambda i, j: (layer, 0, nf + j))],
        out_specs=pl.BlockSpec((tm, tf), lambda i, j: (i, j)),
        out_shape=jax.ShapeDtypeStruct((TOKENS, D_FF), BF16),
        compiler_params=_params("parallel", "arbitrary"),
        name="ffn_up",
    )(h, w_in, w_in)


def _ffn_down_kernel(a_ref, w_ref, x_ref, gate_ref, o_ref, w_scr):
    @pl.when(pl.program_id(1) == 0)
    def _():
        w_scr[...] = w_ref[...].astype(BF16)

    acc = jnp.dot(a_ref[...], w_scr[...], preferred_element_type=F32)
    o_ref[...] = x_ref[...] + gate_ref[0] * acc


def _ffn_down(a, w, x, mod, layer):
    tm, tn = 512, 512
    base = (layer * 6 + 5) * MOD_ROWS
    return pl.pallas_call(
        _ffn_down_kernel, grid=(D_MODEL // tn, TOKENS // tm),
        in_specs=[pl.BlockSpec((tm, D_FF), lambda j, i: (i, 0)),
                  pl.BlockSpec((None, D_FF, tn), lambda j, i: (layer, 0, j)),
                  pl.BlockSpec((tm, tn), lambda j, i: (i, j)),
                  pl.BlockSpec((1, 1, tn), lambda j, i: (base + _mod_row(i, tm), 0, j))],
        out_specs=pl.BlockSpec((tm, tn), lambda j, i: (i, j)),
        out_shape=jax.ShapeDtypeStruct((TOKENS, D_MODEL), F32),
        scratch_shapes=[pltpu.VMEM((D_FF, tn), BF16)],
        input_output_aliases={2: 0},
        compiler_params=_params("arbitrary", "arbitrary"),
        name="ffn_down",
    )(a, w, x, mod)


def _norm_mod_kernel(x_ref, g_ref, sc_ref, sh_ref, h_ref):
    h_ref[...] = _norm_mod(x_ref[...], g_ref[...], sc_ref[0], sh_ref[0])


def _norm_mod_rows(x, g, mod, layer):
    tm = 512
    return pl.pallas_call(
        _norm_mod_kernel, grid=(TOKENS // tm,),
        in_specs=[pl.BlockSpec((tm, D_MODEL), lambda i: (i, 0)),
                  _row_spec(), _mod_spec(layer, 1, tm), _mod_spec(layer, 0, tm)],
        out_specs=pl.BlockSpec((tm, D_MODEL), lambda i: (i, 0)),
        out_shape=jax.ShapeDtypeStruct((TOKENS, D_MODEL), BF16),
        compiler_params=_params("parallel"),
        name="norm_mod",
    )(x, g, mod, mod)


def _final_norm_kernel(x_ref, g_ref, o_ref):
    x = x_ref[...]
    ms = jnp.mean(x * x, axis=-1, keepdims=True)
    o_ref[...] = (x * lax.rsqrt(ms + EPS)) * g_ref[...]


def _final_norm(x, g, row0, rows):
    tm = 512
    t0 = row0 // tm
    return pl.pallas_call(
        _final_norm_kernel, grid=(rows // tm,),
        in_specs=[pl.BlockSpec((tm, D_MODEL), lambda i: (t0 + i, 0)), _row_spec()],
        out_specs=pl.BlockSpec((tm, D_MODEL), lambda i: (i, 0)),
        out_shape=jax.ShapeDtypeStruct((rows, D_MODEL), F32),
        compiler_params=_params("parallel"),
        name="final_norm",
    )(x, g)


def _chunk_cumsum(g, sub, reverse):
    rows, lanes = g.shape
    tiles = rows // SUBLANES
    x = g.reshape(tiles, SUBLANES, lanes)
    s = 1
    while s < SUBLANES:
        if reverse:
            x = x + jnp.where(sub < SUBLANES - s, pltpu.roll(x, SUBLANES - s, 1), 0.0)
        else:
            x = x + jnp.where(sub >= s, pltpu.roll(x, s, 1), 0.0)
        s *= 2
    parts = [None] * tiles
    carry = None
    for j in (range(tiles - 1, -1, -1) if reverse else range(tiles)):
        xj = x[j] if carry is None else x[j] + carry
        parts[j] = xj
        carry = xj[0:1, :] if reverse else xj[SUBLANES - 1:SUBLANES, :]
    return jnp.concatenate(parts, axis=0)


def _gla_kernel(*refs, variant, layer, n_seq, seq_len, hb, dv, has_s0):
    refs = list(refs)
    if variant == "hgrn":
        q_ref, ff_ref, fb_ref, v_ref, gate_ref, lb_ref, gain_ref = refs[:7]
        refs = refs[7:]
        gate_in = (ff_ref, fb_ref)
    else:
        q_ref, k_ref, v_ref, gate_ref, lr_ref, wa_ref, ba_ref, gain_ref = refs[:8]
        refs = refs[8:]
    s0_ref = refs.pop(0) if has_s0 else None
    refs.pop(0)
    out_ref, st_ref = refs[:2]
    o_scr, s_scr = refs[2:4]
    g_scr = refs[4:]

    t = seq_len
    n = t // SCAN_CHUNK
    half = SCAN_CHUNK // 2
    sub = lax.broadcasted_iota(jnp.int32, (SCAN_CHUNK // SUBLANES, SUBLANES, A_DK), 1)
    ri = lax.broadcasted_iota(jnp.int32, (SCAN_CHUNK, SCAN_CHUNK), 0)
    ci = lax.broadcasted_iota(jnp.int32, (SCAN_CHUNK, SCAN_CHUNK), 1)
    masks = (ci <= ri, ci >= ri)
    nt_dims = (((1,), (1,)), ((), ()))
    tn_dims = (((0,), (0,)), ((), ()))
    gain = gain_ref[...]

    rows_total = n_seq * t
    if variant == "hgrn":
        lbr = lb_ref[...]
        ex = jnp.exp(lbr - jnp.max(lbr, axis=0, keepdims=True))
        lb = jnp.sum(ex[:layer + 1], axis=0, keepdims=True) / jnp.sum(ex, axis=0, keepdims=True)
    else:
        def decay_tile(i, carry):
            rows = pl.ds(pl.multiple_of(i * TIME_TILE, TIME_TILE), TIME_TILE)
            lr = lr_ref[rows, :].astype(BF16)
            for d in range(2):
                z = jnp.dot(lr, wa_ref[d].astype(BF16), preferred_element_type=F32) + ba_ref[d:d + 1, :]
                g_scr[d][rows, :] = -_softplus(-z) / GLA_GATE_NORM
            return carry

        lax.fori_loop(0, rows_total // TIME_TILE, decay_tile, 0)

    def chunk_step(c, s, h, d):
        rows = pl.ds(pl.multiple_of(s * t + c * SCAN_CHUNK, SCAN_CHUNK), SCAN_CHUNK)
        kc = slice(h * A_DK, (h + 1) * A_DK)
        vc = slice(h * dv, (h + 1) * dv)
        chain = (s * hb + h) * 2 + d
        q = q_ref[rows, kc] * (A_DK ** -0.5)
        if variant == "hgrn":
            lbh = lb[:, kc]
            f = lbh + (1.0 - lbh) * _sigmoid(gate_in[d][rows, kc])
            k = 1.0 - f
            g = jnp.log(f)
        else:
            k = k_ref[rows, kc]
            g = g_scr[d][rows, kc]
        b = _chunk_cumsum(g, sub, d == 1)
        tot = b[0:1, :] if d == 1 else b[SCAN_CHUNK - 1:SCAN_CHUNK, :]
        mid = b[half:half + 1, :] if d == 1 else b[half - 1:half, :]
        qs = (q * jnp.exp(b)).astype(BF16)
        qa = (q * jnp.exp(b - mid)).astype(BF16)
        ka = (k * jnp.exp(mid - b)).astype(BF16)
        kt = (k * jnp.exp(tot - b)).astype(BF16)
        vb = v_ref[rows, vc].astype(BF16)
        s_t = s_scr[chain]
        att = lax.dot_general(qa, ka, nt_dims, preferred_element_type=F32)
        att = jnp.where(masks[d], att, 0.0).astype(BF16)
        o = jnp.dot(jnp.concatenate([qs, att], axis=1),
                    jnp.concatenate([s_t.astype(BF16), vb], axis=0), preferred_element_type=F32)
        decay = jnp.broadcast_to(jnp.exp(tot), (A_DK, A_DK)).T
        if dv != A_DK:
            decay = jnp.concatenate([decay] * (dv // A_DK), axis=1)
        s_scr[chain] = decay * s_t + lax.dot_general(kt, vb, tn_dims, preferred_element_type=F32)
        return rows, vc, o

    def finish(rows, vc, o):
        o = o + o_scr[rows, vc]
        ms = jnp.mean(o * o, axis=-1, keepdims=True)
        o = (o * lax.rsqrt(ms + EPS)) * gain
        out_ref[rows, vc] = (o * _silu(gate_ref[rows, vc])).astype(out_ref.dtype)

    def first_half(i, carry):
        for s in range(n_seq):
            for h in range(hb):
                for d in range(2):
                    rows, vc, o = chunk_step(i if d == 0 else n - 1 - i, s, h, d)
                    o_scr[rows, vc] = o
        return carry

    def second_half(i, carry):
        for s in range(n_seq):
            for h in range(hb):
                for d in range(2):
                    finish(*chunk_step(i if d == 0 else n - 1 - i, s, h, d))
        return carry

    for s in range(n_seq):
        for h in range(hb):
            for d in range(2):
                chain = (s * hb + h) * 2 + d
                s_scr[chain] = s0_ref[s, 0, d, h] if has_s0 else jnp.zeros((A_DK, dv), F32)
    lax.fori_loop(0, n // 2, first_half, 0)
    lax.fori_loop(n // 2, n, second_half, 0)
    for s in range(n_seq):
        for h in range(hb):
            for d in range(2):
                st_ref[s, 0, d, h] = s_scr[(s * hb + h) * 2 + d]


def _gla_scan(variant, layer, e, proj, lowrank, prev, s0, n_seq, seq_len, row0, small, seq_blk, hb):
    t = seq_len
    rows = seq_blk * t
    tb0 = row0 // rows
    heads, dv = (A_HEADS, A_DV) if variant == "hgrn" else (B_HEADS, B_DV)

    def col_spec(width, col0):
        blk0 = col0 // (hb * width)
        return pl.BlockSpec((rows, hb * width), lambda s, h: (tb0 + s, blk0 + h))

    if variant == "hgrn":
        o_q, o_ff, o_fb, o_v, o_g = 0, A_KDIM, 2 * A_KDIM, 3 * A_KDIM, 3 * A_KDIM + A_VDIM
        in_specs = [col_spec(A_DK, o_q), col_spec(A_DK, o_ff), col_spec(A_DK, o_fb),
                    col_spec(dv, o_v), col_spec(dv, o_g),
                    pl.BlockSpec((DEPTH + 1, hb * A_DK), lambda s, h: (0, h)),
                    pl.BlockSpec((1, dv), lambda s, h: (0, 0))]
        args = [proj] * 5 + [small["lb"], small["gain"]]
        out_col0 = 0
        decay_scratch = []
    else:
        base = 3 * A_KDIM + 2 * A_VDIM
        o_q, o_k, o_v, o_g = base, base + B_KDIM, base + 2 * B_KDIM, base + 2 * B_KDIM + B_VDIM
        in_specs = [col_spec(B_DK, o_q), col_spec(B_DK, o_k),
                    col_spec(dv, o_v), col_spec(dv, o_g),
                    pl.BlockSpec((rows, 2 * GLA_RANK), lambda s, h: (tb0 + s, 0)),
                    pl.BlockSpec((2, 2 * GLA_RANK, hb * B_DK), lambda s, h: (0, 0, h)),
                    pl.BlockSpec((2, hb * B_DK), lambda s, h: (0, h)),
                    pl.BlockSpec((1, dv), lambda s, h: (0, 0))]
        args = [proj] * 4 + [lowrank, small["wa"], small["ba"], small["gain"]]
        out_col0 = A_VDIM
        decay_scratch = [pltpu.VMEM((rows, hb * B_DK), F32)] * 2
    has_s0 = s0 is not None
    if has_s0:
        in_specs.append(pl.BlockSpec((seq_blk, 1, 2, hb, A_DK, dv), lambda s, h: (s, e, 0, h, 0, 0)))
        args.append(s0)
    in_specs.append(pl.BlockSpec(memory_space=pl.ANY))
    aliases = {len(args): 0}
    args.append(prev)
    out_blk0 = out_col0 // (hb * dv)
    merged, states = pl.pallas_call(
        functools.partial(_gla_kernel, variant=variant, layer=layer, n_seq=seq_blk, seq_len=t, hb=hb, dv=dv,
                          has_s0=has_s0),
        grid=(n_seq // seq_blk, heads // hb),
        in_specs=in_specs,
        out_specs=[pl.BlockSpec((rows, hb * dv), lambda s, h: (tb0 + s, out_blk0 + h)),
                   pl.BlockSpec((seq_blk, 1, 2, hb, A_DK, dv), lambda s, h: (s, 0, 0, h, 0, 0))],
        out_shape=[jax.ShapeDtypeStruct((TOKENS, A_VDIM + B_VDIM), BF16),
                   jax.ShapeDtypeStruct((n_seq, 1, 2, heads, A_DK, dv), F32)],
        scratch_shapes=[pltpu.VMEM((rows, hb * dv), F32), pltpu.VMEM((seq_blk * hb * 2, A_DK, dv), F32)]
                       + decay_scratch,
        input_output_aliases=aliases,
        compiler_params=_params("parallel", "arbitrary"),
        name=f"{variant}_scan",
    )(*args)
    return merged, states


def _affine_scan_in_tiles(a, u, sub, reverse):
    s = 1
    while s < SUBLANES:
        if reverse:
            ok, shift = sub < SUBLANES - s, SUBLANES - s
        else:
            ok, shift = sub >= s, s
        u = u + a * jnp.where(ok, pltpu.roll(u, shift, 1), 0.0)
        a = a * jnp.where(ok, pltpu.roll(a, shift, 1), 1.0)
        s *= 2
    return a, u


def _lru_kernel(*refs, seq_len, row_len, has_s0):
    refs = list(refs)
    xr_ref, gate_ref, cw_ref, cb_ref, rgw_ref, rgb_ref, igw_ref, igb_ref, lam_ref = refs[:9]
    refs = refs[9:]
    s0_ref = refs.pop(0) if has_s0 else None
    refs.pop(0)
    out_ref, st_ref = refs[:2]
    xc_scr, y_scr, a_scr, u_scr = refs[2:]

    t = seq_len
    w = LRU_BW
    tl = TIME_TILE
    n_tiles = t // tl
    n_groups = tl // SUBLANES
    pos = lax.broadcasted_iota(jnp.int32, (tl, w), 0) % row_len
    sub = lax.broadcasted_iota(jnp.int32, (n_groups, SUBLANES, w), 1)
    sub2 = lax.broadcasted_iota(jnp.int32, (n_groups // SUBLANES, SUBLANES, w), 1)
    gpos = lax.broadcasted_iota(jnp.int32, (n_groups, w), 0)
    spread = jnp.where(lax.broadcasted_iota(jnp.int32, (tl, 4 * n_groups), 0) // SUBLANES
                       == lax.broadcasted_iota(jnp.int32, (tl, 4 * n_groups), 1) % n_groups, 1.0, 0.0).astype(BF16)
    cw = cw_ref[...]

    def conv(x):
        xc = cb_ref[...] + jnp.where(pos >= 2, pltpu.roll(x, 2, 0), 0.0) * cw[0:1, :]
        xc = xc + jnp.where(pos >= 1, pltpu.roll(x, 1, 0), 0.0) * cw[1:2, :]
        xc = xc + x * cw[2:3, :]
        return xc + jnp.where(pos < row_len - 1, pltpu.roll(x, tl - 1, 0), 0.0) * cw[3:4, :]

    def tile_pass(d):
        nsp = -LRU_C * _softplus(-lam_ref[d:d + 1, :])
        edge = SUBLANES - 1 if d == 0 else 0

        def tile_body(i, hprev):
            tile = i if d == 0 else n_tiles - 1 - i
            r0 = pl.multiple_of(tile * tl, tl)
            rows = pl.ds(r0, tl)
            if d == 0:
                xc = conv(xr_ref[rows, :])
                xc_scr[rows, :] = xc
            else:
                xc = xc_scr[rows, :]
            xcb = xc.astype(BF16)
            r = _sigmoid(jnp.dot(xcb, rgw_ref[d, 0].astype(BF16), preferred_element_type=F32)
                         + rgb_ref[d:d + 1, :])
            ig = _sigmoid(jnp.dot(xcb, igw_ref[d, 0].astype(BF16), preferred_element_type=F32)
                          + igb_ref[d:d + 1, :])
            log_a = r * nsp
            a = jnp.exp(log_a)
            u = jnp.sqrt(-jnp.tanh(log_a) * (a * a + 1.0)) * (ig * xc)
            a, u = _affine_scan_in_tiles(a.reshape(n_groups, SUBLANES, w), u.reshape(n_groups, SUBLANES, w),
                                         sub, d == 1)
            a = a.reshape(tl, w)
            u = u.reshape(tl, w)
            n2 = n_groups // SUBLANES
            edge_rows = pl.ds(edge, n_groups, stride=SUBLANES)
            for lt in range(w // LANES):
                a_scr[lt] = a[:, lt * LANES:(lt + 1) * LANES]
                u_scr[lt] = u[:, lt * LANES:(lt + 1) * LANES]
            ae = jnp.concatenate([a_scr[lt, edge_rows, :] for lt in range(w // LANES)], axis=1)
            ue = jnp.concatenate([u_scr[lt, edge_rows, :] for lt in range(w // LANES)], axis=1)
            a2, u2 = _affine_scan_in_tiles(ae.reshape(n2, SUBLANES, w), ue.reshape(n2, SUBLANES, w),
                                           sub2, d == 1)
            pa, pu = [None] * n2, [None] * n2
            ca = cu = None
            for j in (range(n2 - 1, -1, -1) if d == 1 else range(n2)):
                aj, uj = a2[j], u2[j]
                if ca is not None:
                    uj = uj + aj * cu
                    aj = aj * ca
                pa[j], pu[j] = aj, uj
                ca, cu = aj[edge:edge + 1, :], uj[edge:edge + 1, :]
            c = jnp.concatenate(pu, axis=0) + jnp.concatenate(pa, axis=0) * hprev
            if d == 0:
                cin = jnp.where(gpos >= 1, pltpu.roll(c, 1, 0), hprev)
            else:
                cin = jnp.where(gpos < n_groups - 1, pltpu.roll(c, n_groups - 1, 0), hprev)
            c1 = cin.astype(BF16)
            r1 = cin - c1.astype(F32)
            c2 = r1.astype(BF16)
            c3 = (r1 - c2.astype(F32)).astype(BF16)
            cin_rows = jnp.dot(spread, jnp.concatenate([c1, c2, c3, jnp.zeros_like(c3)], axis=0),
                               preferred_element_type=F32)
            hs = u + a * cin_rows
            if d == 0:
                y_scr[rows, :] = hs
            else:
                y = y_scr[rows, :] + hs
                out_ref[rows, :] = (y * _gelu_tanh(gate_ref[rows, :])).astype(out_ref.dtype)
            return c[n_groups - 1:n_groups, :] if d == 0 else c[0:1, :]

        h0 = s0_ref[0, 0, d:d + 1, :] if has_s0 else jnp.zeros((1, w), F32)
        st_ref[0, 0, d:d + 1, :] = lax.fori_loop(0, n_tiles, tile_body, h0)

    tile_pass(0)
    tile_pass(1)


def _lru_scan(e, proj, prev, s0, n_seq, seq_len, row_len, row0, p):
    t = seq_len
    tb0 = row0 // t
    has_s0 = s0 is not None
    in_specs = [pl.BlockSpec((t, LRU_BW), lambda s, n: (tb0 + s, n)),
                pl.BlockSpec((t, LRU_BW), lambda s, n: (tb0 + s, LRU_BLOCKS + n)),
                pl.BlockSpec((CONV_W, LRU_BW), lambda s, n: (0, n)),
                pl.BlockSpec((1, LRU_BW), lambda s, n: (0, n)),
                pl.BlockSpec((2, 1, LRU_BW, LRU_BW), lambda s, n: (0, n, 0, 0)),
                pl.BlockSpec((2, LRU_BW), lambda s, n: (0, n)),
                pl.BlockSpec((2, 1, LRU_BW, LRU_BW), lambda s, n: (0, n, 0, 0)),
                pl.BlockSpec((2, LRU_BW), lambda s, n: (0, n)),
                pl.BlockSpec((2, LRU_BW), lambda s, n: (0, n))]
    args = [proj, proj, p["conv_w"], p["conv_b"], p["rg_w"], p["rg_b"], p["ig_w"], p["ig_b"], p["lam"]]
    if has_s0:
        in_specs.append(pl.BlockSpec((1, 1, 2, LRU_BW), lambda s, n: (s, e, 0, n)))
        args.append(s0)
    in_specs.append(pl.BlockSpec(memory_space=pl.ANY))
    aliases = {len(args): 0}
    args.append(prev)
    return pl.pallas_call(
        functools.partial(_lru_kernel, seq_len=t, row_len=row_len, has_s0=has_s0),
        grid=(n_seq, LRU_BLOCKS),
        in_specs=in_specs,
        out_specs=[pl.BlockSpec((t, LRU_BW), lambda s, n: (tb0 + s, n)),
                   pl.BlockSpec((1, 1, 2, LRU_BW), lambda s, n: (s, 0, 0, n))],
        out_shape=[jax.ShapeDtypeStruct((TOKENS, LRU_WIDTH), BF16),
                   jax.ShapeDtypeStruct((n_seq, 1, 2, LRU_WIDTH), F32)],
        scratch_shapes=[pltpu.VMEM((t, LRU_BW), F32), pltpu.VMEM((t, LRU_BW), F32),
                        pltpu.VMEM((LRU_BW // LANES, TIME_TILE, LANES), F32),
                        pltpu.VMEM((LRU_BW // LANES, TIME_TILE, LANES), F32)],
        input_output_aliases=aliases,
        compiler_params=_params("parallel", "arbitrary"),
        name="lru_scan",
    )(*args)


def kernel(x_prompt, x_sample, state_hgrn, state_gla, state_rglru, c, c_ctx, norm1_g, norm2_g, w_mod, b_mod,
           ffn_w_in, ffn_w_out, hgrn_lower_bounds, even_w_in, gla_w_alpha, gla_b_alpha, hgrn_norm_g,
           gla_norm_g, even_w_out, odd_w_in, conv_w, conv_b, rg_w, rg_b, ig_w, ig_b, lru_lambda, odd_w_out,
           final_norm_g):
    c3 = jnp.concatenate([c_ctx[None, :], c, jnp.zeros((MOD_ROWS - 1 - DEC_BATCH, D_MODEL), F32)], axis=0)
    mod = _modulation(c3, w_mod, b_mod)
    x, h = _stack_tokens(x_prompt.reshape(N_PROMPT, D_MODEL), x_sample.reshape(N_SAMPLE, D_MODEL),
                         norm1_g[0][None, :], mod)

    new_hgrn, new_gla, new_lru = [], [], []
    y_prompt = y_sample = None
    for l in range(DEPTH):
        e = l // 2
        if l % 2 == 0:
            proj, lowrank = _inproj(h, even_w_in, e, EVEN_MAIN, even_w_in[e][:, EVEN_MAIN:])
            wa = jnp.zeros((2, 2 * GLA_RANK, B_KDIM), F32)
            wa = wa.at[0, :GLA_RANK].set(gla_w_alpha[e, 0]).at[1, GLA_RANK:].set(gla_w_alpha[e, 1])
            small_a = {"lb": hgrn_lower_bounds, "gain": hgrn_norm_g[e][None, :]}
            small_b = {"wa": wa, "ba": gla_b_alpha[e], "gain": gla_norm_g[e][None, :]}
            merged = jnp.zeros((TOKENS, A_VDIM + B_VDIM), BF16)
            merged, sa = _gla_scan("hgrn", l, e, proj, None, merged, None, BATCH, SEQ, 0, small_a, 4, 1)
            merged, sb = _gla_scan("gla", l, e, proj, lowrank, merged, None, BATCH, SEQ, 0, small_b, 4, 1)
            merged, _ = _gla_scan("hgrn", l, e, proj, None, merged, state_hgrn, DEC_BATCH, DEC_SEQ, N_PROMPT,
                                  small_a, 2, 2)
            merged, _ = _gla_scan("gla", l, e, proj, lowrank, merged, state_gla, DEC_BATCH, DEC_SEQ, N_PROMPT,
                                  small_b, 2, 1)
            new_hgrn.append(sa)
            new_gla.append(sb)
            w_out = even_w_out
        else:
            proj = _inproj(h, odd_w_in, e, 2 * LRU_WIDTH)
            p = {"conv_w": conv_w[e], "conv_b": conv_b[e][None, :], "rg_w": rg_w[e], "rg_b": rg_b[e],
                 "ig_w": ig_w[e], "ig_b": ig_b[e], "lam": lru_lambda[e]}
            merged = jnp.zeros((TOKENS, LRU_WIDTH), BF16)
            merged, sr = _lru_scan(e, proj, merged, None, BATCH, SEQ, SEQ, 0, p)
            merged, _ = _lru_scan(e, proj, merged, state_rglru, DEC_BATCH, DEC_SEQ, GRID_W, N_PROMPT, p)
            new_lru.append(sr)
            w_out = odd_w_out
        x, h = _outproj(merged, w_out, e, x, mod, l, norm2_g[l][None, :])
        hidden = _ffn_up(h, l, ffn_w_in)
        x = _ffn_down(hidden, ffn_w_out, x, mod, l)
        if l + 1 < DEPTH:
            h = _norm_mod_rows(x, norm1_g[l + 1][None, :], mod, l + 1)
    gf = final_norm_g[None, :]
    y_prompt = _final_norm(x, gf, 0, N_PROMPT)
    y_sample = _final_norm(x, gf, N_PROMPT, N_SAMPLE)
    dt = x_prompt.dtype
    return (y_prompt.reshape(BATCH, SEQ, D_MODEL), y_sample.reshape(DEC_BATCH, DEC_SEQ, D_MODEL),
            jnp.concatenate(new_hgrn, axis=1).astype(dt), jnp.concatenate(new_gla, axis=1).astype(dt),
            jnp.concatenate(new_lru, axis=1).astype(dt))
```

```python
import functools

import jax
import jax.numpy as jnp
from jax import lax
from jax.experimental import pallas as pl
from jax.experimental.pallas import tpu as pltpu

F32 = jnp.float32
BF16 = jnp.bfloat16

D_MODEL = 2048
BATCH = 16
SEQ = 256
DEPTH = 2
DEC_BATCH = 2
DEC_SEQ = 2048
GRID_W = 64
N_EVEN = (DEPTH + 1) // 2
N_ODD = DEPTH // 2
A_HEADS = 8
A_DK = 128
A_DV = 128
A_KDIM = A_HEADS * A_DK
A_VDIM = A_HEADS * A_DV
B_HEADS = 4
B_DK = 128
B_DV = 256
B_KDIM = B_HEADS * B_DK
B_VDIM = B_HEADS * B_DV
GLA_RANK = 16
GLA_GATE_NORM = 16.0
SCAN_CHUNK = 64
EVEN_MAIN = 3 * A_KDIM + 2 * A_VDIM + 2 * B_KDIM + 2 * B_VDIM
LRU_WIDTH = D_MODEL
LRU_BLOCKS = 8
LRU_BW = LRU_WIDTH // LRU_BLOCKS
LRU_C = 8.0
CONV_W = 4
D_FF = (8 * D_MODEL + 3 * 256 - 1) // (3 * 256) * 256
EPS = 1e-6

N_PROMPT = BATCH * SEQ
N_SAMPLE = DEC_BATCH * DEC_SEQ
TOKENS = N_PROMPT + N_SAMPLE
MOD_ROWS = 8
SUBLANES = 8
LANES = 128
TIME_TILE = 256
VMEM_LIMIT = 56 * 1024 * 1024


def _sigmoid(x):
    return 0.5 * (jnp.tanh(0.5 * x) + 1.0)


def _silu(x):
    return x * _sigmoid(x)


def _softplus(x):
    return jnp.maximum(x, 0.0) + jnp.log1p(jnp.exp(-jnp.abs(x)))


def _gelu_tanh(x):
    c = 0.7978845608028654
    return x * (0.5 * (1.0 + jnp.tanh(c * (x + 0.044715 * (x * x * x)))))


def _mod_row(tile, tile_rows):
    r0 = tile * tile_rows
    return jnp.where(r0 < N_PROMPT, 0, 1 + (r0 - N_PROMPT) // DEC_SEQ)


def _mod_spec(layer, which, tile_rows, tile0=0, width=D_MODEL):
    base = (layer * 6 + which) * MOD_ROWS
    if width == D_MODEL:
        return pl.BlockSpec((1, 1, width), lambda *ids: (base + _mod_row(tile0 + ids[0], tile_rows), 0, 0))
    return pl.BlockSpec((1, 1, width), lambda i, j: (base + _mod_row(tile0 + i, tile_rows), 0, j))


def _row_spec():
    return pl.BlockSpec((1, D_MODEL), lambda *ids: (0, 0))


def _norm_mod(x, g, sc, sh):
    ms = jnp.mean(x * x, axis=-1, keepdims=True)
    y = (x * lax.rsqrt(ms + EPS)) * g
    return (y * (1.0 + sc) + sh).astype(BF16)


def _params(*sem):
    return pltpu.CompilerParams(dimension_semantics=sem, vmem_limit_bytes=VMEM_LIMIT)


def _stack_kernel(p_ref, s_ref, g_ref, sc_ref, sh_ref, o_ref, h_ref, *, prompt_tiles):
    def emit(x):
        o_ref[...] = x
        h_ref[...] = _norm_mod(x, g_ref[...], sc_ref[0], sh_ref[0])

    @pl.when(pl.program_id(0) < prompt_tiles)
    def _():
        emit(p_ref[...])

    @pl.when(pl.program_id(0) >= prompt_tiles)
    def _():
        emit(s_ref[...])


def _stack_tokens(xp, xs, g, mod):
    tm = 512
    pt = N_PROMPT // tm
    return pl.pallas_call(
        functools.partial(_stack_kernel, prompt_tiles=pt), grid=(TOKENS // tm,),
        in_specs=[pl.BlockSpec((tm, D_MODEL), lambda i: (jnp.minimum(i, pt - 1), 0)),
                  pl.BlockSpec((tm, D_MODEL), lambda i: (jnp.maximum(i - pt, 0), 0)),
                  _row_spec(), _mod_spec(0, 1, tm), _mod_spec(0, 0, tm)],
        out_specs=[pl.BlockSpec((tm, D_MODEL), lambda i: (i, 0)),
                   pl.BlockSpec((tm, D_MODEL), lambda i: (i, 0))],
        out_shape=[jax.ShapeDtypeStruct((TOKENS, D_MODEL), F32),
                   jax.ShapeDtypeStruct((TOKENS, D_MODEL), BF16)],
        compiler_params=_params("arbitrary"),
        name="stack_tokens",
    )(xp, xs, g, mod, mod)


def _mod_kernel(c_ref, w_ref, b_ref, o_ref):
    s = _silu(c_ref[...]).astype(BF16)
    o_ref[0, 0] = jnp.dot(s, w_ref[0].astype(BF16), preferred_element_type=F32) + b_ref[0]


def _modulation(c3, w_mod, b_mod):
    tn = 1024
    nb = D_MODEL // tn
    out = pl.pallas_call(
        _mod_kernel,
        grid=(DEPTH, 6, nb),
        in_specs=[pl.BlockSpec((MOD_ROWS, D_MODEL), lambda l, k, j: (0, 0)),
                  pl.BlockSpec((1, D_MODEL, tn), lambda l, k, j: (l, 0, k * nb + j)),
                  pl.BlockSpec((1, 1, tn), lambda l, k, j: (l, 0, k * nb + j))],
        out_specs=pl.BlockSpec((1, 1, MOD_ROWS, tn), lambda l, k, j: (l, k, 0, j)),
        out_shape=jax.ShapeDtypeStruct((DEPTH, 6, MOD_ROWS, D_MODEL), F32),
        compiler_params=_params("arbitrary", "arbitrary", "arbitrary"),
        name="modulation",
    )(c3, w_mod, b_mod.reshape(DEPTH, 1, 6 * D_MODEL))
    return out.reshape(DEPTH * 6 * MOD_ROWS, 1, D_MODEL)


INPROJ_TN = 512


def _relayout_kernel(wt_ref, wlt_ref, o_ref, ol_ref):
    o_ref[...] = wt_ref[...].T.astype(BF16)

    @pl.when(pl.program_id(0) == 0)
    def _():
        ol_ref[...] = wlt_ref[...].T.astype(BF16)


def _feature_major_to_bf16(wt, widx, n_main, n_low):
    tn = INPROJ_TN
    return pl.pallas_call(
        _relayout_kernel, grid=(n_main // tn,),
        in_specs=[pl.BlockSpec((None, tn, D_MODEL), lambda j: (widx, j, 0)),
                  pl.BlockSpec((None, n_low, D_MODEL), lambda j: (widx, n_main // n_low, 0))],
        out_specs=[pl.BlockSpec((None, D_MODEL, tn), lambda j: (0, 0, j)),
                   pl.BlockSpec((D_MODEL, n_low), lambda j: (0, 0))],
        out_shape=[jax.ShapeDtypeStruct((1, D_MODEL, n_main), BF16),
                   jax.ShapeDtypeStruct((D_MODEL, n_low), BF16)],
        compiler_params=_params("arbitrary"),
        name="weight_relayout",
    )(wt, wt)


def _inproj_kernel(h_ref, w_ref, o_ref):
    o_ref[...] = jnp.dot(h_ref[...], w_ref[...].astype(BF16), preferred_element_type=F32).astype(o_ref.dtype)


def _inproj_lowrank_kernel(h_ref, w_ref, wl_ref, o_ref, ol_ref):
    @pl.when(pl.program_id(1) == 0)
    def _():
        ol_ref[...] = jnp.dot(h_ref[...], wl_ref[...].astype(BF16), preferred_element_type=F32)

    o_ref[...] = jnp.dot(h_ref[...], w_ref[...].astype(BF16), preferred_element_type=F32).astype(o_ref.dtype)


def _inproj(h, w, widx, groups, out_dtype, w_low=None):
    tm, tn = 2048, INPROJ_TN
    (b0, n0), (b1, _) = (groups + [(0, 0)])[:2]
    n_blocks = sum(n for _, n in groups)

    def wblock(j):
        return jnp.where(j < n0, b0 + j, b1 + j - n0) if len(groups) > 1 else b0 + j

    in_specs = [pl.BlockSpec((tm, D_MODEL), lambda i, j: (i, 0)),
                pl.BlockSpec((None, D_MODEL, tn), lambda i, j: (widx, 0, wblock(j)))]
    out_specs = pl.BlockSpec((tm, tn), lambda i, j: (i, j))
    out_shape = jax.ShapeDtypeStruct((TOKENS, n_blocks * tn), out_dtype)
    args = [h, w]
    body = _inproj_kernel
    if w_low is not None:
        nl = w_low.shape[1]
        in_specs.append(pl.BlockSpec((D_MODEL, nl), lambda i, j: (0, 0)))
        out_specs = [out_specs, pl.BlockSpec((tm, nl), lambda i, j: (i, 0))]
        out_shape = [out_shape, jax.ShapeDtypeStruct((TOKENS, nl), F32)]
        args.append(w_low)
        body = _inproj_lowrank_kernel
    return pl.pallas_call(
        body, grid=(TOKENS // tm, n_blocks), in_specs=in_specs, out_specs=out_specs, out_shape=out_shape,
        compiler_params=_params("parallel", "arbitrary"),
        name="inproj",
    )(*args)


def _outproj_kernel(a_ref, w_ref, x_ref, gate_ref, g_ref, sc_ref, sh_ref, o_ref, h_ref, w_scr):
    @pl.when(pl.program_id(0) == 0)
    def _():
        w_scr[...] = w_ref[...].astype(BF16)

    acc = jnp.dot(a_ref[...], w_scr[...], preferred_element_type=F32)
    y = x_ref[...] + gate_ref[0] * acc
    o_ref[...] = y
    h_ref[...] = _norm_mod(y, g_ref[...], sc_ref[0], sh_ref[0])


def _outproj(a, w, widx, x, mod, layer, g2):
    tm = 512
    k = a.shape[1]
    return pl.pallas_call(
        _outproj_kernel, grid=(TOKENS // tm,),
        in_specs=[pl.BlockSpec((tm, k), lambda i: (i, 0)),
                  pl.BlockSpec((None, k, D_MODEL), lambda i: (widx, 0, 0), pipeline_mode=pl.Buffered(1)),
                  pl.BlockSpec((tm, D_MODEL), lambda i: (i, 0)),
                  _mod_spec(layer, 2, tm), _row_spec(), _mod_spec(layer, 4, tm), _mod_spec(layer, 3, tm)],
        out_specs=[pl.BlockSpec((tm, D_MODEL), lambda i: (i, 0)),
                   pl.BlockSpec((tm, D_MODEL), lambda i: (i, 0))],
        out_shape=[jax.ShapeDtypeStruct((TOKENS, D_MODEL), F32),
                   jax.ShapeDtypeStruct((TOKENS, D_MODEL), BF16)],
        scratch_shapes=[pltpu.VMEM((k, D_MODEL), BF16)],
        input_output_aliases={2: 0},
        compiler_params=_params("arbitrary"),
        name="outproj",
    )(a, w, x, mod, g2, mod, mod)


def _ffn_up_kernel(h_ref, wg_ref, wu_ref, o_ref):
    h = h_ref[...]
    gt = jnp.dot(h, wg_ref[...].astype(BF16), preferred_element_type=F32)
    up = jnp.dot(h, wu_ref[...].astype(BF16), preferred_element_type=F32)
    o_ref[...] = (_silu(gt) * up).astype(BF16)


def _ffn_up(h, layer, w_in):
    tm, tf = 2048, 512
    nf = D_FF // tf
    return pl.pallas_call(
        _ffn_up_kernel, grid=(TOKENS // tm, nf),
        in_specs=[pl.BlockSpec((tm, D_MODEL), lambda i, j: (i, 0)),
                  pl.BlockSpec((None, D_MODEL, tf), lambda i, j: (layer, 0, j)),
                  pl.BlockSpec((None, D_MODEL, tf), lambda i, j: (layer, 0, nf + j))],
        out_specs=pl.BlockSpec((tm, tf), lambda i, j: (i, j)),
        out_shape=jax.ShapeDtypeStruct((TOKENS, D_FF), BF16),
        compiler_params=_params("parallel", "arbitrary"),
        name="ffn_up",
    )(h, w_in, w_in)


def _ffn_down_kernel(a_ref, w_ref, x_ref, gate_ref, o_ref, w_scr):
    @pl.when(pl.program_id(1) == 0)
    def _():
        w_scr[...] = w_ref[...].astype(BF16)

    acc = jnp.dot(a_ref[...], w_scr[...], preferred_element_type=F32)
    o_ref[...] = x_ref[...] + gate_ref[0] * acc


def _ffn_down(a, w, x, mod, layer):
    tm, tn = 512, 512
    base = (layer * 6 + 5) * MOD_ROWS
    return pl.pallas_call(
        _ffn_down_kernel, grid=(D_MODEL // tn, TOKENS // tm),
        in_specs=[pl.BlockSpec((tm, D_FF), lambda j, i: (i, 0)),
                  pl.BlockSpec((None, D_FF, tn), lambda j, i: (layer, 0, j)),
                  pl.BlockSpec((tm, tn), lambda j, i: (i, j)),
                  pl.BlockSpec((1, 1, tn), lambda j, i: (base + _mod_row(i, tm), 0, j))],
        out_specs=pl.BlockSpec((tm, tn), lambda j, i: (i, j)),
        out_shape=jax.ShapeDtypeStruct((TOKENS, D_MODEL), F32),
        scratch_shapes=[pltpu.VMEM((D_FF, tn), BF16)],
        input_output_aliases={2: 0},
        compiler_params=_params("arbitrary", "arbitrary"),
        name="ffn_down",
    )(a, w, x, mod)


def _norm_mod_kernel(x_ref, g_ref, sc_ref, sh_ref, h_ref):
    h_ref[...] = _norm_mod(x_ref[...], g_ref[...], sc_ref[0], sh_ref[0])


def _norm_mod_rows(x, g, mod, layer):
    tm = 512
    return pl.pallas_call(
        _norm_mod_kernel, grid=(TOKENS // tm,),
        in_specs=[pl.BlockSpec((tm, D_MODEL), lambda i: (i, 0)),
                  _row_spec(), _mod_spec(layer, 1, tm), _mod_spec(layer, 0, tm)],
        out_specs=pl.BlockSpec((tm, D_MODEL), lambda i: (i, 0)),
        out_shape=jax.ShapeDtypeStruct((TOKENS, D_MODEL), BF16),
        compiler_params=_params("parallel"),
        name="norm_mod",
    )(x, g, mod, mod)


def _final_norm_kernel(x_ref, g_ref, o_ref):
    x = x_ref[...]
    ms = jnp.mean(x * x, axis=-1, keepdims=True)
    o_ref[...] = (x * lax.rsqrt(ms + EPS)) * g_ref[...]


def _final_norm(x, g, row0, rows):
    tm = 512
    t0 = row0 // tm
    return pl.pallas_call(
        _final_norm_kernel, grid=(rows // tm,),
        in_specs=[pl.BlockSpec((tm, D_MODEL), lambda i: (t0 + i, 0)), _row_spec()],
        out_specs=pl.BlockSpec((tm, D_MODEL), lambda i: (i, 0)),
        out_shape=jax.ShapeDtypeStruct((rows, D_MODEL), F32),
        compiler_params=_params("parallel"),
        name="final_norm",
    )(x, g)


def _chunk_cumsum(g, sub, reverse):
    rows, lanes = g.shape
    tiles = rows // SUBLANES
    x = g.reshape(tiles, SUBLANES, lanes)
    s = 1
    while s < SUBLANES:
        if reverse:
            x = x + jnp.where(sub < SUBLANES - s, pltpu.roll(x, SUBLANES - s, 1), 0.0)
        else:
            x = x + jnp.where(sub >= s, pltpu.roll(x, s, 1), 0.0)
        s *= 2
    parts = [None] * tiles
    carry = None
    for j in (range(tiles - 1, -1, -1) if reverse else range(tiles)):
        xj = x[j] if carry is None else x[j] + carry
        parts[j] = xj
        carry = xj[0:1, :] if reverse else xj[SUBLANES - 1:SUBLANES, :]
    return jnp.concatenate(parts, axis=0)


def _gla_kernel(*refs, variant, layer, n_seq, seq_len, hb, dv, has_s0):
    refs = list(refs)
    if variant == "hgrn":
        q_ref, ff_ref, fb_ref, v_ref, gate_ref, lb_ref, gain_ref = refs[:7]
        refs = refs[7:]
        gate_in = (ff_ref, fb_ref)
    else:
        q_ref, k_ref, v_ref, gate_ref, lr_ref, wa_ref, ba_ref, gain_ref = refs[:8]
        refs = refs[8:]
    s0_ref = refs.pop(0) if has_s0 else None
    refs.pop(0)
    out_ref, st_ref = refs[:2]
    o_scr, s_scr = refs[2:4]
    g_scr = refs[4:]

    t = seq_len
    n = t // SCAN_CHUNK
    half = SCAN_CHUNK // 2
    sub = lax.broadcasted_iota(jnp.int32, (SCAN_CHUNK // SUBLANES, SUBLANES, A_DK), 1)
    ri = lax.broadcasted_iota(jnp.int32, (SCAN_CHUNK, SCAN_CHUNK), 0)
    ci = lax.broadcasted_iota(jnp.int32, (SCAN_CHUNK, SCAN_CHUNK), 1)
    masks = (ci <= ri, ci >= ri)
    nt_dims = (((1,), (1,)), ((), ()))
    tn_dims = (((0,), (0,)), ((), ()))
    gain = gain_ref[...]

    rows_total = n_seq * t
    if variant == "hgrn":
        lbr = lb_ref[...]
        ex = jnp.exp(lbr - jnp.max(lbr, axis=0, keepdims=True))
        lb = jnp.sum(ex[:layer + 1], axis=0, keepdims=True) / jnp.sum(ex, axis=0, keepdims=True)
    else:
        def decay_tile(i, carry):
            rows = pl.ds(pl.multiple_of(i * TIME_TILE, TIME_TILE), TIME_TILE)
            lr = lr_ref[rows, :].astype(BF16)
            for d in range(2):
                z = jnp.dot(lr, wa_ref[d].astype(BF16), preferred_element_type=F32) + ba_ref[d:d + 1, :]
                g_scr[d][rows, :] = -_softplus(-z) / GLA_GATE_NORM
            return carry

        lax.fori_loop(0, rows_total // TIME_TILE, decay_tile, 0)

    def chunk_step(c, s, h, d):
        rows = pl.ds(pl.multiple_of(s * t + c * SCAN_CHUNK, SCAN_CHUNK), SCAN_CHUNK)
        kc = slice(h * A_DK, (h + 1) * A_DK)
        vc = slice(h * dv, (h + 1) * dv)
        chain = (s * hb + h) * 2 + d
        q = q_ref[rows, kc] * (A_DK ** -0.5)
        if variant == "hgrn":
            lbh = lb[:, kc]
            f = lbh + (1.0 - lbh) * _sigmoid(gate_in[d][rows, kc])
            k = 1.0 - f
            g = jnp.log(f)
        else:
            k = k_ref[rows, kc]
            g = g_scr[d][rows, kc]
        b = _chunk_cumsum(g, sub, d == 1)
        tot = b[0:1, :] if d == 1 else b[SCAN_CHUNK - 1:SCAN_CHUNK, :]
        mid = b[half:half + 1, :] if d == 1 else b[half - 1:half, :]
        qs = (q * jnp.exp(b)).astype(BF16)
        qa = (q * jnp.exp(b - mid)).astype(BF16)
        ka = (k * jnp.exp(mid - b)).astype(BF16)
        kt = (k * jnp.exp(tot - b)).astype(BF16)
        vb = v_ref[rows, vc].astype(BF16)
        s_t = s_scr[chain]
        att = lax.dot_general(qa, ka, nt_dims, preferred_element_type=F32)
        att = jnp.where(masks[d], att, 0.0).astype(BF16)
        o = jnp.dot(jnp.concatenate([qs, att], axis=1),
                    jnp.concatenate([s_t.astype(BF16), vb], axis=0), preferred_element_type=F32)
        decay = jnp.broadcast_to(jnp.exp(tot), (A_DK, A_DK)).T
        if dv != A_DK:
            decay = jnp.concatenate([decay] * (dv // A_DK), axis=1)
        s_scr[chain] = decay * s_t + lax.dot_general(kt, vb, tn_dims, preferred_element_type=F32)
        return rows, vc, o

    def finish(rows, vc, o):
        o = o + o_scr[rows, vc]
        ms = jnp.mean(o * o, axis=-1, keepdims=True)
        o = (o * lax.rsqrt(ms + EPS)) * gain
        out_ref[rows, vc] = (o * _silu(gate_ref[rows, vc].astype(F32))).astype(out_ref.dtype)

    def first_half(i, carry):
        for s in range(n_seq):
            for h in range(hb):
                for d in range(2):
                    rows, vc, o = chunk_step(i if d == 0 else n - 1 - i, s, h, d)
                    o_scr[rows, vc] = o
        return carry

    def second_half(i, carry):
        for s in range(n_seq):
            for h in range(hb):
                for d in range(2):
                    finish(*chunk_step(i if d == 0 else n - 1 - i, s, h, d))
        return carry

    for s in range(n_seq):
        for h in range(hb):
            for d in range(2):
                chain = (s * hb + h) * 2 + d
                s_scr[chain] = s0_ref[s, 0, d, h] if has_s0 else jnp.zeros((A_DK, dv), F32)
    lax.fori_loop(0, n // 2, first_half, 0)
    lax.fori_loop(n // 2, n, second_half, 0)
    for s in range(n_seq):
        for h in range(hb):
            for d in range(2):
                st_ref[s, 0, d, h] = s_scr[(s * hb + h) * 2 + d]


def _gla_scan(variant, layer, e, proj, proj16, lowrank, prev, s0, n_seq, seq_len, row0, small, seq_blk, hb):
    t = seq_len
    rows = seq_blk * t
    tb0 = row0 // rows
    heads, dv = (A_HEADS, A_DV) if variant == "hgrn" else (B_HEADS, B_DV)

    def col_spec(width, col0):
        blk0 = col0 // (hb * width)
        return pl.BlockSpec((rows, hb * width), lambda s, h: (tb0 + s, blk0 + h))

    if variant == "hgrn":
        o_q, o_ff, o_fb = 0, A_KDIM, 2 * A_KDIM
        o_v, o_g = 0, A_VDIM
        in_specs = [col_spec(A_DK, o_q), col_spec(A_DK, o_ff), col_spec(A_DK, o_fb),
                    col_spec(dv, o_v), col_spec(dv, o_g),
                    pl.BlockSpec((DEPTH + 1, hb * A_DK), lambda s, h: (0, h)),
                    pl.BlockSpec((1, dv), lambda s, h: (0, 0))]
        args = [proj] * 3 + [proj16] * 2 + [small["lb"], small["gain"]]
        out_col0 = 0
        decay_scratch = []
    else:
        o_q, o_k = 3 * A_KDIM, 3 * A_KDIM + B_KDIM
        o_v, o_g = 2 * A_VDIM, 2 * A_VDIM + B_VDIM
        in_specs = [col_spec(B_DK, o_q), col_spec(B_DK, o_k),
                    col_spec(dv, o_v), col_spec(dv, o_g),
                    pl.BlockSpec((rows, 2 * GLA_RANK), lambda s, h: (tb0 + s, 0)),
                    pl.BlockSpec((2, 2 * GLA_RANK, hb * B_DK), lambda s, h: (0, 0, h)),
                    pl.BlockSpec((2, hb * B_DK), lambda s, h: (0, h)),
                    pl.BlockSpec((1, dv), lambda s, h: (0, 0))]
        args = [proj] * 2 + [proj16] * 2 + [lowrank, small["wa"], small["ba"], small["gain"]]
        out_col0 = A_VDIM
        decay_scratch = [pltpu.VMEM((rows, hb * B_DK), F32)] * 2
    has_s0 = s0 is not None
    if has_s0:
        in_specs.append(pl.BlockSpec((seq_blk, 1, 2, hb, A_DK, dv), lambda s, h: (s, e, 0, h, 0, 0)))
        args.append(s0)
    in_specs.append(pl.BlockSpec(memory_space=pl.ANY))
    aliases = {len(args): 0}
    args.append(prev)
    out_blk0 = out_col0 // (hb * dv)
    merged, states = pl.pallas_call(
        functools.partial(_gla_kernel, variant=variant, layer=layer, n_seq=seq_blk, seq_len=t, hb=hb, dv=dv,
                          has_s0=has_s0),
        grid=(n_seq // seq_blk, heads // hb),
        in_specs=in_specs,
        out_specs=[pl.BlockSpec((rows, hb * dv), lambda s, h: (tb0 + s, out_blk0 + h)),
                   pl.BlockSpec((seq_blk, 1, 2, hb, A_DK, dv), lambda s, h: (s, 0, 0, h, 0, 0))],
        out_shape=[jax.ShapeDtypeStruct((TOKENS, A_VDIM + B_VDIM), BF16),
                   jax.ShapeDtypeStruct((n_seq, 1, 2, heads, A_DK, dv), F32)],
        scratch_shapes=[pltpu.VMEM((rows, hb * dv), F32), pltpu.VMEM((seq_blk * hb * 2, A_DK, dv), F32)]
                       + decay_scratch,
        input_output_aliases=aliases,
        compiler_params=_params("parallel", "arbitrary"),
        name=f"{variant}_scan",
    )(*args)
    return merged, states


def _affine_scan_in_tiles(a, u, sub, reverse):
    s = 1
    while s < SUBLANES:
        if reverse:
            ok, shift = sub < SUBLANES - s, SUBLANES - s
        else:
            ok, shift = sub >= s, s
        u = u + a * jnp.where(ok, pltpu.roll(u, shift, 1), 0.0)
        a = a * jnp.where(ok, pltpu.roll(a, shift, 1), 1.0)
        s *= 2
    return a, u


def _lru_kernel(*refs, seq_len, row_len, has_s0):
    refs = list(refs)
    xr_ref, gate_ref, cw_ref, cb_ref, rgw_ref, rgb_ref, igw_ref, igb_ref, lam_ref = refs[:9]
    refs = refs[9:]
    s0_ref = refs.pop(0) if has_s0 else None
    refs.pop(0)
    out_ref, st_ref = refs[:2]
    xc_scr, y_scr, a_scr, u_scr = refs[2:]

    t = seq_len
    w = LRU_BW
    tl = TIME_TILE
    n_tiles = t // tl
    n_groups = tl // SUBLANES
    pos = lax.broadcasted_iota(jnp.int32, (tl, w), 0) % row_len
    sub = lax.broadcasted_iota(jnp.int32, (n_groups, SUBLANES, w), 1)
    sub2 = lax.broadcasted_iota(jnp.int32, (n_groups // SUBLANES, SUBLANES, w), 1)
    gpos = lax.broadcasted_iota(jnp.int32, (n_groups, w), 0)
    spread = jnp.where(lax.broadcasted_iota(jnp.int32, (tl, 4 * n_groups), 0) // SUBLANES
                       == lax.broadcasted_iota(jnp.int32, (tl, 4 * n_groups), 1) % n_groups, 1.0, 0.0).astype(BF16)
    cw = cw_ref[...]

    def conv(x):
        xc = cb_ref[...] + jnp.where(pos >= 2, pltpu.roll(x, 2, 0), 0.0) * cw[0:1, :]
        xc = xc + jnp.where(pos >= 1, pltpu.roll(x, 1, 0), 0.0) * cw[1:2, :]
        xc = xc + x * cw[2:3, :]
        return xc + jnp.where(pos < row_len - 1, pltpu.roll(x, tl - 1, 0), 0.0) * cw[3:4, :]

    def tile_pass(d):
        nsp = -LRU_C * _softplus(-lam_ref[d:d + 1, :])
        edge = SUBLANES - 1 if d == 0 else 0

        def tile_body(i, hprev):
            tile = i if d == 0 else n_tiles - 1 - i
            r0 = pl.multiple_of(tile * tl, tl)
            rows = pl.ds(r0, tl)
            if d == 0:
                xc = conv(xr_ref[rows, :])
                xc_scr[rows, :] = xc
            else:
                xc = xc_scr[rows, :]
            xcb = xc.astype(BF16)
            r = _sigmoid(jnp.dot(xcb, rgw_ref[d, 0].astype(BF16), preferred_element_type=F32)
                         + rgb_ref[d:d + 1, :])
            ig = _sigmoid(jnp.dot(xcb, igw_ref[d, 0].astype(BF16), preferred_element_type=F32)
                          + igb_ref[d:d + 1, :])
            log_a = r * nsp
            a = jnp.exp(log_a)
            u = jnp.sqrt(-jnp.tanh(log_a) * (a * a + 1.0)) * (ig * xc)
            a, u = _affine_scan_in_tiles(a.reshape(n_groups, SUBLANES, w), u.reshape(n_groups, SUBLANES, w),
                                         sub, d == 1)
            a = a.reshape(tl, w)
            u = u.reshape(tl, w)
            n2 = n_groups // SUBLANES
            edge_rows = pl.ds(edge, n_groups, stride=SUBLANES)
            for lt in range(w // LANES):
                a_scr[lt] = a[:, lt * LANES:(lt + 1) * LANES]
                u_scr[lt] = u[:, lt * LANES:(lt + 1) * LANES]
            ae = jnp.concatenate([a_scr[lt, edge_rows, :] for lt in range(w // LANES)], axis=1)
            ue = jnp.concatenate([u_scr[lt, edge_rows, :] for lt in range(w // LANES)], axis=1)
            a2, u2 = _affine_scan_in_tiles(ae.reshape(n2, SUBLANES, w), ue.reshape(n2, SUBLANES, w),
                                           sub2, d == 1)
            pa, pu = [None] * n2, [None] * n2
            ca = cu = None
            for j in (range(n2 - 1, -1, -1) if d == 1 else range(n2)):
                aj, uj = a2[j], u2[j]
                if ca is not None:
                    uj = uj + aj * cu
                    aj = aj * ca
                pa[j], pu[j] = aj, uj
                ca, cu = aj[edge:edge + 1, :], uj[edge:edge + 1, :]
            c = jnp.concatenate(pu, axis=0) + jnp.concatenate(pa, axis=0) * hprev
            if d == 0:
                cin = jnp.where(gpos >= 1, pltpu.roll(c, 1, 0), hprev)
            else:
                cin = jnp.where(gpos < n_groups - 1, pltpu.roll(c, n_groups - 1, 0), hprev)
            c1 = cin.astype(BF16)
            r1 = cin - c1.astype(F32)
            c2 = r1.astype(BF16)
            c3 = (r1 - c2.astype(F32)).astype(BF16)
            cin_rows = jnp.dot(spread, jnp.concatenate([c1, c2, c3, jnp.zeros_like(c3)], axis=0),
                               preferred_element_type=F32)
            hs = u + a * cin_rows
            if d == 0:
                y_scr[rows, :] = hs
            else:
                y = y_scr[rows, :] + hs
                out_ref[rows, :] = (y * _gelu_tanh(gate_ref[rows, :].astype(F32))).astype(out_ref.dtype)
            return c[n_groups - 1:n_groups, :] if d == 0 else c[0:1, :]

        h0 = s0_ref[0, 0, d:d + 1, :] if has_s0 else jnp.zeros((1, w), F32)
        st_ref[0, 0, d:d + 1, :] = lax.fori_loop(0, n_tiles, tile_body, h0)

    tile_pass(0)
    tile_pass(1)


def _lru_scan(e, proj, proj16, prev, s0, n_seq, seq_len, row_len, row0, p):
    t = seq_len
    tb0 = row0 // t
    has_s0 = s0 is not None
    in_specs = [pl.BlockSpec((t, LRU_BW), lambda s, n: (tb0 + s, n)),
                pl.BlockSpec((t, LRU_BW), lambda s, n: (tb0 + s, n)),
                pl.BlockSpec((CONV_W, LRU_BW), lambda s, n: (0, n)),
                pl.BlockSpec((1, LRU_BW), lambda s, n: (0, n)),
                pl.BlockSpec((2, 1, LRU_BW, LRU_BW), lambda s, n: (0, n, 0, 0)),
                pl.BlockSpec((2, LRU_BW), lambda s, n: (0, n)),
                pl.BlockSpec((2, 1, LRU_BW, LRU_BW), lambda s, n: (0, n, 0, 0)),
                pl.BlockSpec((2, LRU_BW), lambda s, n: (0, n)),
                pl.BlockSpec((2, LRU_BW), lambda s, n: (0, n))]
    args = [proj, proj16, p["conv_w"], p["conv_b"], p["rg_w"], p["rg_b"], p["ig_w"], p["ig_b"], p["lam"]]
    if has_s0:
        in_specs.append(pl.BlockSpec((1, 1, 2, LRU_BW), lambda s, n: (s, e, 0, n)))
        args.append(s0)
    in_specs.append(pl.BlockSpec(memory_space=pl.ANY))
    aliases = {len(args): 0}
    args.append(prev)
    return pl.pallas_call(
        functools.partial(_lru_kernel, seq_len=t, row_len=row_len, has_s0=has_s0),
        grid=(n_seq, LRU_BLOCKS),
        in_specs=in_specs,
        out_specs=[pl.BlockSpec((t, LRU_BW), lambda s, n: (tb0 + s, n)),
                   pl.BlockSpec((1, 1, 2, LRU_BW), lambda s, n: (s, 0, 0, n))],
        out_shape=[jax.ShapeDtypeStruct((TOKENS, LRU_WIDTH), BF16),
                   jax.ShapeDtypeStruct((n_seq, 1, 2, LRU_WIDTH), F32)],
        scratch_shapes=[pltpu.VMEM((t, LRU_BW), F32), pltpu.VMEM((t, LRU_BW), F32),
                        pltpu.VMEM((LRU_BW // LANES, TIME_TILE, LANES), F32),
                        pltpu.VMEM((LRU_BW // LANES, TIME_TILE, LANES), F32)],
        input_output_aliases=aliases,
        compiler_params=_params("parallel", "arbitrary"),
        name="lru_scan",
    )(*args)


def kernel(x_prompt, x_sample, state_hgrn, state_gla, state_rglru, c, c_ctx, norm1_g, norm2_g, w_mod, b_mod,
           ffn_w_in, ffn_w_out, hgrn_lower_bounds, even_w_in, gla_w_alpha, gla_b_alpha, hgrn_norm_g,
           gla_norm_g, even_w_out, odd_w_in, conv_w, conv_b, rg_w, rg_b, ig_w, ig_b, lru_lambda, odd_w_out,
           final_norm_g):
    c3 = jnp.concatenate([c_ctx[None, :], c, jnp.zeros((MOD_ROWS - 1 - DEC_BATCH, D_MODEL), F32)], axis=0)
    mod = _modulation(c3, w_mod, b_mod)
    x, h = _stack_tokens(x_prompt.reshape(N_PROMPT, D_MODEL), x_sample.reshape(N_SAMPLE, D_MODEL),
                         norm1_g[0][None, :], mod)

    new_hgrn, new_gla, new_lru = [], [], []
    y_prompt = y_sample = None
    for l in range(DEPTH):
        e = l // 2
        if l % 2 == 0:
            w16, w16_low = _feature_major_to_bf16(jnp.swapaxes(even_w_in, 1, 2), e, EVEN_MAIN, 2 * GLA_RANK)
            nb = A_KDIM // INPROJ_TN
            proj, lowrank = _inproj(h, w16, 0, [(0, 3 * nb), (5 * nb, nb)], F32, w16_low)
            proj16 = _inproj(h, w16, 0, [(3 * nb, 2 * nb), (6 * nb, 2 * nb)], BF16)
            wa = jnp.zeros((2, 2 * GLA_RANK, B_KDIM), F32)
            wa = wa.at[0, :GLA_RANK].set(gla_w_alpha[e, 0]).at[1, GLA_RANK:].set(gla_w_alpha[e, 1])
            small_a = {"lb": hgrn_lower_bounds, "gain": hgrn_norm_g[e][None, :]}
            small_b = {"wa": wa, "ba": gla_b_alpha[e], "gain": gla_norm_g[e][None, :]}
            merged = jnp.zeros((TOKENS, A_VDIM + B_VDIM), BF16)
            merged, sa = _gla_scan("hgrn", l, e, proj, proj16, None, merged, None, BATCH, SEQ, 0, small_a, 4, 1)
            merged, sb = _gla_scan("gla", l, e, proj, proj16, lowrank, merged, None, BATCH, SEQ, 0, small_b, 4, 1)
            merged, _ = _gla_scan("hgrn", l, e, proj, proj16, None, merged, state_hgrn, DEC_BATCH, DEC_SEQ, N_PROMPT,
                                  small_a, 2, 2)
            merged, _ = _gla_scan("gla", l, e, proj, proj16, lowrank, merged, state_gla, DEC_BATCH, DEC_SEQ, N_PROMPT,
                                  small_b, 2, 1)
            new_hgrn.append(sa)
            new_gla.append(sb)
            w_out = even_w_out
        else:
            nb = LRU_WIDTH // INPROJ_TN
            proj = _inproj(h, odd_w_in, e, [(0, nb)], F32)
            proj16 = _inproj(h, odd_w_in, e, [(nb, nb)], BF16)
            p = {"conv_w": conv_w[e], "conv_b": conv_b[e][None, :], "rg_w": rg_w[e], "rg_b": rg_b[e],
                 "ig_w": ig_w[e], "ig_b": ig_b[e], "lam": lru_lambda[e]}
            merged = jnp.zeros((TOKENS, LRU_WIDTH), BF16)
            merged, sr = _lru_scan(e, proj, proj16, merged, None, BATCH, SEQ, SEQ, 0, p)
            merged, _ = _lru_scan(e, proj, proj16, merged, state_rglru, DEC_BATCH, DEC_SEQ, GRID_W, N_PROMPT, p)
            new_lru.append(sr)
            w_out = odd_w_out
        x, h = _outproj(merged, w_out, e, x, mod, l, norm2_g[l][None, :])
        hidden = _ffn_up(h, l, ffn_w_in)
        x = _ffn_down(hidden, ffn_w_out, x, mod, l)
        if l + 1 < DEPTH:
            h = _norm_mod_rows(x, norm1_g[l + 1][None, :], mod, l + 1)
    gf = final_norm_g[None, :]
    y_prompt = _final_norm(x, gf, 0, N_PROMPT)
    y_sample = _final_norm(x, gf, N_PROMPT, N_SAMPLE)
    dt = x_prompt.dtype
    return (y_prompt.reshape(BATCH, SEQ, D_MODEL), y_sample.reshape(DEC_BATCH, DEC_SEQ, D_MODEL),
            jnp.concatenate(new_hgrn, axis=1).astype(dt), jnp.concatenate(new_gla, axis=1).astype(dt),
            jnp.concatenate(new_lru, axis=1).astype(dt))
```

```python
import functools

import jax
import jax.numpy as jnp
from jax import lax
from jax.experimental import pallas as pl
from jax.experimental.pallas import tpu as pltpu

F32 = jnp.float32
BF16 = jnp.bfloat16

D_MODEL = 2048
BATCH = 16
SEQ = 256
DEPTH = 2
DEC_BATCH = 2
DEC_SEQ = 2048
GRID_W = 64
N_EVEN = (DEPTH + 1) // 2
N_ODD = DEPTH // 2
A_HEADS = 8
A_DK = 128
A_DV = 128
A_KDIM = A_HEADS * A_DK
A_VDIM = A_HEADS * A_DV
B_HEADS = 4
B_DK = 128
B_DV = 256
B_KDIM = B_HEADS * B_DK
B_VDIM = B_HEADS * B_DV
GLA_RANK = 16
GLA_GATE_NORM = 16.0
SCAN_CHUNK = 64
EVEN_MAIN = 3 * A_KDIM + 2 * A_VDIM + 2 * B_KDIM + 2 * B_VDIM
LRU_WIDTH = D_MODEL
LRU_BLOCKS = 8
LRU_BW = LRU_WIDTH // LRU_BLOCKS
LRU_C = 8.0
CONV_W = 4
D_FF = (8 * D_MODEL + 3 * 256 - 1) // (3 * 256) * 256
EPS = 1e-6
LOG2_E = 1.4426950408889634

N_PROMPT = BATCH * SEQ
N_SAMPLE = DEC_BATCH * DEC_SEQ
TOKENS = N_PROMPT + N_SAMPLE
MOD_ROWS = 8
SUBLANES = 8
LANES = 128
TIME_TILE = 256
VMEM_LIMIT = 56 * 1024 * 1024


def _sigmoid(x):
    return 0.5 * (jnp.tanh(0.5 * x) + 1.0)


def _silu(x):
    return x * _sigmoid(x)


def _softplus(x):
    return jnp.maximum(x, 0.0) + jnp.log1p(jnp.exp(-jnp.abs(x)))


def _gelu_tanh(x):
    c = 0.7978845608028654
    return x * (0.5 * (1.0 + jnp.tanh(c * (x + 0.044715 * (x * x * x)))))


def _mod_row(tile, tile_rows):
    r0 = tile * tile_rows
    return jnp.where(r0 < N_PROMPT, 0, 1 + (r0 - N_PROMPT) // DEC_SEQ)


def _mod_spec(layer, which, tile_rows, tile0=0, width=D_MODEL):
    base = (layer * 6 + which) * MOD_ROWS
    if width == D_MODEL:
        return pl.BlockSpec((1, 1, width), lambda *ids: (base + _mod_row(tile0 + ids[0], tile_rows), 0, 0))
    return pl.BlockSpec((1, 1, width), lambda i, j: (base + _mod_row(tile0 + i, tile_rows), 0, j))


def _row_spec():
    return pl.BlockSpec((1, D_MODEL), lambda *ids: (0, 0))


def _norm_mod(x, g, sc, sh):
    ms = jnp.mean(x * x, axis=-1, keepdims=True)
    y = (x * lax.rsqrt(ms + EPS)) * g
    return (y * (1.0 + sc) + sh).astype(BF16)


def _params(*sem):
    return pltpu.CompilerParams(dimension_semantics=sem, vmem_limit_bytes=VMEM_LIMIT)


def _stack_kernel(p_ref, s_ref, g_ref, sc_ref, sh_ref, o_ref, h_ref, *, prompt_tiles):
    def emit(x):
        o_ref[...] = x
        h_ref[...] = _norm_mod(x, g_ref[...], sc_ref[0], sh_ref[0])

    @pl.when(pl.program_id(0) < prompt_tiles)
    def _():
        emit(p_ref[...])

    @pl.when(pl.program_id(0) >= prompt_tiles)
    def _():
        emit(s_ref[...])


def _stack_tokens(xp, xs, g, mod):
    tm = 512
    pt = N_PROMPT // tm
    return pl.pallas_call(
        functools.partial(_stack_kernel, prompt_tiles=pt), grid=(TOKENS // tm,),
        in_specs=[pl.BlockSpec((tm, D_MODEL), lambda i: (jnp.minimum(i, pt - 1), 0)),
                  pl.BlockSpec((tm, D_MODEL), lambda i: (jnp.maximum(i - pt, 0), 0)),
                  _row_spec(), _mod_spec(0, 1, tm), _mod_spec(0, 0, tm)],
        out_specs=[pl.BlockSpec((tm, D_MODEL), lambda i: (i, 0)),
                   pl.BlockSpec((tm, D_MODEL), lambda i: (i, 0))],
        out_shape=[jax.ShapeDtypeStruct((TOKENS, D_MODEL), F32),
                   jax.ShapeDtypeStruct((TOKENS, D_MODEL), BF16)],
        compiler_params=_params("arbitrary"),
        name="stack_tokens",
    )(xp, xs, g, mod, mod)


def _mod_kernel(c_ref, w_ref, b_ref, o_ref):
    s = _silu(c_ref[...]).astype(BF16)
    o_ref[0, 0] = jnp.dot(s, w_ref[0].astype(BF16), preferred_element_type=F32) + b_ref[0]


def _modulation(c3, w_mod, b_mod):
    tn = 1024
    nb = D_MODEL // tn
    out = pl.pallas_call(
        _mod_kernel,
        grid=(DEPTH, 6, nb),
        in_specs=[pl.BlockSpec((MOD_ROWS, D_MODEL), lambda l, k, j: (0, 0)),
                  pl.BlockSpec((1, D_MODEL, tn), lambda l, k, j: (l, 0, k * nb + j)),
                  pl.BlockSpec((1, 1, tn), lambda l, k, j: (l, 0, k * nb + j))],
        out_specs=pl.BlockSpec((1, 1, MOD_ROWS, tn), lambda l, k, j: (l, k, 0, j)),
        out_shape=jax.ShapeDtypeStruct((DEPTH, 6, MOD_ROWS, D_MODEL), F32),
        compiler_params=_params("arbitrary", "arbitrary", "arbitrary"),
        name="modulation",
    )(c3, w_mod, b_mod.reshape(DEPTH, 1, 6 * D_MODEL))
    return out.reshape(DEPTH * 6 * MOD_ROWS, 1, D_MODEL)


INPROJ_TN = 512


def _relayout_kernel(wt_ref, wlt_ref, o_ref, ol_ref):
    o_ref[...] = wt_ref[...].T.astype(BF16)

    @pl.when(pl.program_id(0) == 0)
    def _():
        ol_ref[...] = wlt_ref[...].T.astype(BF16)


def _feature_major_to_bf16(wt, widx, n_main, n_low):
    tn = INPROJ_TN
    return pl.pallas_call(
        _relayout_kernel, grid=(n_main // tn,),
        in_specs=[pl.BlockSpec((None, tn, D_MODEL), lambda j: (widx, j, 0)),
                  pl.BlockSpec((None, n_low, D_MODEL), lambda j: (widx, n_main // n_low, 0))],
        out_specs=[pl.BlockSpec((None, D_MODEL, tn), lambda j: (0, 0, j)),
                   pl.BlockSpec((D_MODEL, n_low), lambda j: (0, 0))],
        out_shape=[jax.ShapeDtypeStruct((1, D_MODEL, n_main), BF16),
                   jax.ShapeDtypeStruct((D_MODEL, n_low), BF16)],
        compiler_params=_params("arbitrary"),
        name="weight_relayout",
    )(wt, wt)


def _inproj_kernel(h_ref, w_ref, o_ref):
    o_ref[...] = jnp.dot(h_ref[...], w_ref[...].astype(BF16), preferred_element_type=F32).astype(o_ref.dtype)


def _inproj_lowrank_kernel(h_ref, w_ref, wl_ref, o_ref, ol_ref):
    @pl.when(pl.program_id(1) == 0)
    def _():
        ol_ref[...] = jnp.dot(h_ref[...], wl_ref[...].astype(BF16), preferred_element_type=F32)

    o_ref[...] = jnp.dot(h_ref[...], w_ref[...].astype(BF16), preferred_element_type=F32).astype(o_ref.dtype)


def _inproj(h, w, widx, groups, out_dtype, w_low=None):
    tm, tn = 2048, INPROJ_TN
    (b0, n0), (b1, _) = (groups + [(0, 0)])[:2]
    n_blocks = sum(n for _, n in groups)

    def wblock(j):
        return jnp.where(j < n0, b0 + j, b1 + j - n0) if len(groups) > 1 else b0 + j

    in_specs = [pl.BlockSpec((tm, D_MODEL), lambda i, j: (i, 0)),
                pl.BlockSpec((None, D_MODEL, tn), lambda i, j: (widx, 0, wblock(j)))]
    out_specs = pl.BlockSpec((tm, tn), lambda i, j: (i, j))
    out_shape = jax.ShapeDtypeStruct((TOKENS, n_blocks * tn), out_dtype)
    args = [h, w]
    body = _inproj_kernel
    if w_low is not None:
        nl = w_low.shape[1]
        in_specs.append(pl.BlockSpec((D_MODEL, nl), lambda i, j: (0, 0)))
        out_specs = [out_specs, pl.BlockSpec((tm, nl), lambda i, j: (i, 0))]
        out_shape = [out_shape, jax.ShapeDtypeStruct((TOKENS, nl), F32)]
        args.append(w_low)
        body = _inproj_lowrank_kernel
    return pl.pallas_call(
        body, grid=(TOKENS // tm, n_blocks), in_specs=in_specs, out_specs=out_specs, out_shape=out_shape,
        compiler_params=_params("parallel", "arbitrary"),
        name="inproj",
    )(*args)


def _outproj_kernel(a_ref, w_ref, x_ref, gate_ref, g_ref, sc_ref, sh_ref, o_ref, h_ref, w_scr):
    @pl.when(pl.program_id(0) == 0)
    def _():
        w_scr[...] = w_ref[...].astype(BF16)

    acc = jnp.dot(a_ref[...], w_scr[...], preferred_element_type=F32)
    y = x_ref[...] + gate_ref[0] * acc
    o_ref[...] = y
    h_ref[...] = _norm_mod(y, g_ref[...], sc_ref[0], sh_ref[0])


def _outproj(a, w, widx, x, mod, layer, g2):
    tm = 512
    k = a.shape[1]
    return pl.pallas_call(
        _outproj_kernel, grid=(TOKENS // tm,),
        in_specs=[pl.BlockSpec((tm, k), lambda i: (i, 0)),
                  pl.BlockSpec((None, k, D_MODEL), lambda i: (widx, 0, 0), pipeline_mode=pl.Buffered(1)),
                  pl.BlockSpec((tm, D_MODEL), lambda i: (i, 0)),
                  _mod_spec(layer, 2, tm), _row_spec(), _mod_spec(layer, 4, tm), _mod_spec(layer, 3, tm)],
        out_specs=[pl.BlockSpec((tm, D_MODEL), lambda i: (i, 0)),
                   pl.BlockSpec((tm, D_MODEL), lambda i: (i, 0))],
        out_shape=[jax.ShapeDtypeStruct((TOKENS, D_MODEL), F32),
                   jax.ShapeDtypeStruct((TOKENS, D_MODEL), BF16)],
        scratch_shapes=[pltpu.VMEM((k, D_MODEL), BF16)],
        input_output_aliases={2: 0},
        compiler_params=_params("arbitrary"),
        name="outproj",
    )(a, w, x, mod, g2, mod, mod)


def _ffn_up_kernel(h_ref, wg_ref, wu_ref, o_ref):
    h = h_ref[...]
    gt = jnp.dot(h, wg_ref[...].astype(BF16), preferred_element_type=F32)
    up = jnp.dot(h, wu_ref[...].astype(BF16), preferred_element_type=F32)
    o_ref[...] = (_silu(gt) * up).astype(BF16)


def _ffn_up(h, layer, w_in):
    tm, tf = 2048, 512
    nf = D_FF // tf
    return pl.pallas_call(
        _ffn_up_kernel, grid=(TOKENS // tm, nf),
        in_specs=[pl.BlockSpec((tm, D_MODEL), lambda i, j: (i, 0)),
                  pl.BlockSpec((None, D_MODEL, tf), lambda i, j: (layer, 0, j)),
                  pl.BlockSpec((None, D_MODEL, tf), lambda i, j: (layer, 0, nf + j))],
        out_specs=pl.BlockSpec((tm, tf), lambda i, j: (i, j)),
        out_shape=jax.ShapeDtypeStruct((TOKENS, D_FF), BF16),
        compiler_params=_params("parallel", "arbitrary"),
        name="ffn_up",
    )(h, w_in, w_in)


def _ffn_down_kernel(a_ref, w_ref, x_ref, gate_ref, o_ref, w_scr):
    @pl.when(pl.program_id(1) == 0)
    def _():
        w_scr[...] = w_ref[...].astype(BF16)

    acc = jnp.dot(a_ref[...], w_scr[...], preferred_element_type=F32)
    o_ref[...] = x_ref[...] + gate_ref[0] * acc


def _ffn_down(a, w, x, mod, layer):
    tm, tn = 512, 512
    base = (layer * 6 + 5) * MOD_ROWS
    return pl.pallas_call(
        _ffn_down_kernel, grid=(D_MODEL // tn, TOKENS // tm),
        in_specs=[pl.BlockSpec((tm, D_FF), lambda j, i: (i, 0)),
                  pl.BlockSpec((None, D_FF, tn), lambda j, i: (layer, 0, j)),
                  pl.BlockSpec((tm, tn), lambda j, i: (i, j)),
                  pl.BlockSpec((1, 1, tn), lambda j, i: (base + _mod_row(i, tm), 0, j))],
        out_specs=pl.BlockSpec((tm, tn), lambda j, i: (i, j)),
        out_shape=jax.ShapeDtypeStruct((TOKENS, D_MODEL), F32),
        scratch_shapes=[pltpu.VMEM((D_FF, tn), BF16)],
        input_output_aliases={2: 0},
        compiler_params=_params("arbitrary", "arbitrary"),
        name="ffn_down",
    )(a, w, x, mod)


def _norm_mod_kernel(x_ref, g_ref, sc_ref, sh_ref, h_ref):
    h_ref[...] = _norm_mod(x_ref[...], g_ref[...], sc_ref[0], sh_ref[0])


def _norm_mod_rows(x, g, mod, layer):
    tm = 512
    return pl.pallas_call(
        _norm_mod_kernel, grid=(TOKENS // tm,),
        in_specs=[pl.BlockSpec((tm, D_MODEL), lambda i: (i, 0)),
                  _row_spec(), _mod_spec(layer, 1, tm), _mod_spec(layer, 0, tm)],
        out_specs=pl.BlockSpec((tm, D_MODEL), lambda i: (i, 0)),
        out_shape=jax.ShapeDtypeStruct((TOKENS, D_MODEL), BF16),
        compiler_params=_params("parallel"),
        name="norm_mod",
    )(x, g, mod, mod)


def _final_norm_kernel(x_ref, g_ref, o_ref):
    x = x_ref[...]
    ms = jnp.mean(x * x, axis=-1, keepdims=True)
    o_ref[...] = (x * lax.rsqrt(ms + EPS)) * g_ref[...]


def _final_norm(x, g, row0, rows):
    tm = 512
    t0 = row0 // tm
    return pl.pallas_call(
        _final_norm_kernel, grid=(rows // tm,),
        in_specs=[pl.BlockSpec((tm, D_MODEL), lambda i: (t0 + i, 0)), _row_spec()],
        out_specs=pl.BlockSpec((tm, D_MODEL), lambda i: (i, 0)),
        out_shape=jax.ShapeDtypeStruct((rows, D_MODEL), F32),
        compiler_params=_params("parallel"),
        name="final_norm",
    )(x, g)


def _chunk_cumsum(g, sub, reverse):
    rows, lanes = g.shape
    tiles = rows // SUBLANES
    x = g.reshape(tiles, SUBLANES, lanes)
    s = 1
    while s < SUBLANES:
        if reverse:
            x = x + jnp.where(sub < SUBLANES - s, pltpu.roll(x, SUBLANES - s, 1), 0.0)
        else:
            x = x + jnp.where(sub >= s, pltpu.roll(x, s, 1), 0.0)
        s *= 2
    parts = [None] * tiles
    carry = None
    for j in (range(tiles - 1, -1, -1) if reverse else range(tiles)):
        xj = x[j] if carry is None else x[j] + carry
        parts[j] = xj
        carry = xj[0:1, :] if reverse else xj[SUBLANES - 1:SUBLANES, :]
    return jnp.concatenate(parts, axis=0)


def _gla_kernel(*refs, variant, layer, n_seq, seq_len, hb, dv, has_s0):
    refs = list(refs)
    if variant == "hgrn":
        q_ref, ff_ref, fb_ref, v_ref, gate_ref, lb_ref, gain_ref = refs[:7]
        refs = refs[7:]
        gate_in = (ff_ref, fb_ref)
    else:
        q_ref, k_ref, v_ref, gate_ref, lr_ref, wa_ref, ba_ref, gain_ref = refs[:8]
        refs = refs[8:]
    s0_ref = refs.pop(0) if has_s0 else None
    refs.pop(0)
    out_ref, st_ref = refs[:2]
    o_scr, s_scr = refs[2:4]
    g_scr = refs[4:]

    t = seq_len
    n = t // SCAN_CHUNK
    half = SCAN_CHUNK // 2
    sub = lax.broadcasted_iota(jnp.int32, (SCAN_CHUNK // SUBLANES, SUBLANES, A_DK), 1)
    ri = lax.broadcasted_iota(jnp.int32, (SCAN_CHUNK, SCAN_CHUNK), 0)
    ci = lax.broadcasted_iota(jnp.int32, (SCAN_CHUNK, SCAN_CHUNK), 1)
    masks = (ci <= ri, ci >= ri)
    nt_dims = (((1,), (1,)), ((), ()))
    tn_dims = (((0,), (0,)), ((), ()))
    gain = gain_ref[...]

    rows_total = n_seq * t
    if variant == "hgrn":
        lbr = lb_ref[...]
        ex = jnp.exp(lbr - jnp.max(lbr, axis=0, keepdims=True))
        lb = jnp.sum(ex[:layer + 1], axis=0, keepdims=True) / jnp.sum(ex, axis=0, keepdims=True)
    else:
        def decay_tile(i, carry):
            rows = pl.ds(pl.multiple_of(i * TIME_TILE, TIME_TILE), TIME_TILE)
            lr = lr_ref[rows, :].astype(BF16)
            for d in range(2):
                z = jnp.dot(lr, wa_ref[d].astype(BF16), preferred_element_type=F32) + ba_ref[d:d + 1, :]
                g_scr[d][rows, :] = -_softplus(-z) * (LOG2_E / GLA_GATE_NORM)
            return carry

        lax.fori_loop(0, rows_total // TIME_TILE, decay_tile, 0)

    def chunk_step(c, s, h, d):
        rows = pl.ds(pl.multiple_of(s * t + c * SCAN_CHUNK, SCAN_CHUNK), SCAN_CHUNK)
        kc = slice(h * A_DK, (h + 1) * A_DK)
        vc = slice(h * dv, (h + 1) * dv)
        chain = (s * hb + h) * 2 + d
        q = q_ref[rows, kc] * (A_DK ** -0.5)
        if variant == "hgrn":
            lbh = lb[:, kc]
            f = lbh + (1.0 - lbh) * _sigmoid(gate_in[d][rows, kc])
            k = 1.0 - f
            g = jnp.log(f) * LOG2_E
        else:
            k = k_ref[rows, kc]
            g = g_scr[d][rows, kc]
        b = _chunk_cumsum(g, sub, d == 1)
        tot = b[0:1, :] if d == 1 else b[SCAN_CHUNK - 1:SCAN_CHUNK, :]
        mid = b[half:half + 1, :] if d == 1 else b[half - 1:half, :]
        qs = (q * jnp.exp2(b)).astype(BF16)
        qa = (q * jnp.exp2(b - mid)).astype(BF16)
        ka = (k * jnp.exp2(mid - b)).astype(BF16)
        kt = (k * jnp.exp2(tot - b)).astype(BF16)
        vb = v_ref[rows, vc].astype(BF16)
        s_t = s_scr[chain]
        att = lax.dot_general(qa, ka, nt_dims, preferred_element_type=F32)
        att = jnp.where(masks[d], att, 0.0).astype(BF16)
        o = jnp.dot(jnp.concatenate([qs, att], axis=1),
                    jnp.concatenate([s_t.astype(BF16), vb], axis=0), preferred_element_type=F32)
        decay = jnp.broadcast_to(jnp.exp2(tot), (A_DK, A_DK)).T
        if dv != A_DK:
            decay = jnp.concatenate([decay] * (dv // A_DK), axis=1)
        s_new = decay * s_t + lax.dot_general(kt, vb, tn_dims, preferred_element_type=F32)
        return chain, rows, vc, o, s_new

    def finish(rows, vc, o):
        o = o + o_scr[rows, vc]
        ms = jnp.mean(o * o, axis=-1, keepdims=True)
        o = (o * lax.rsqrt(ms + EPS)) * gain
        return (o * _silu(gate_ref[rows, vc].astype(F32))).astype(out_ref.dtype)

    chains = [(s, h, d) for d in range(2) for s in range(n_seq) for h in range(hb)]

    def first_half(i, carry):
        done = [chunk_step(i if d == 0 else n - 1 - i, s, h, d) for s, h, d in chains]
        for chain, rows, vc, o, s_new in done:
            s_scr[chain] = s_new
            o_scr[rows, vc] = o + 0.0
        return carry

    def second_half(i, carry):
        done = [chunk_step(i if d == 0 else n - 1 - i, s, h, d) for s, h, d in chains]
        outs = [finish(rows, vc, o) for _, rows, vc, o, _ in done]
        for (chain, rows, vc, _, s_new), out in zip(done, outs):
            s_scr[chain] = s_new
            out_ref[rows, vc] = out
        return carry

    for s in range(n_seq):
        for h in range(hb):
            for d in range(2):
                chain = (s * hb + h) * 2 + d
                s_scr[chain] = s0_ref[s, 0, d, h] if has_s0 else jnp.zeros((A_DK, dv), F32)
    lax.fori_loop(0, n // 2, first_half, 0)
    lax.fori_loop(n // 2, n, second_half, 0)
    for s in range(n_seq):
        for h in range(hb):
            for d in range(2):
                st_ref[s, 0, d, h] = s_scr[(s * hb + h) * 2 + d]


def _gla_scan(variant, layer, e, proj, proj16, lowrank, prev, s0, n_seq, seq_len, row0, small, seq_blk, hb):
    t = seq_len
    rows = seq_blk * t
    tb0 = row0 // rows
    heads, dv = (A_HEADS, A_DV) if variant == "hgrn" else (B_HEADS, B_DV)

    def col_spec(width, col0):
        blk0 = col0 // (hb * width)
        return pl.BlockSpec((rows, hb * width), lambda s, h: (tb0 + s, blk0 + h))

    if variant == "hgrn":
        o_q, o_ff, o_fb = 0, A_KDIM, 2 * A_KDIM
        o_v, o_g = 0, A_VDIM
        in_specs = [col_spec(A_DK, o_q), col_spec(A_DK, o_ff), col_spec(A_DK, o_fb),
                    col_spec(dv, o_v), col_spec(dv, o_g),
                    pl.BlockSpec((DEPTH + 1, hb * A_DK), lambda s, h: (0, h)),
                    pl.BlockSpec((1, dv), lambda s, h: (0, 0))]
        args = [proj] * 3 + [proj16] * 2 + [small["lb"], small["gain"]]
        out_col0 = 0
        decay_scratch = []
    else:
        o_q, o_k = 3 * A_KDIM, 3 * A_KDIM + B_KDIM
        o_v, o_g = 2 * A_VDIM, 2 * A_VDIM + B_VDIM
        in_specs = [col_spec(B_DK, o_q), col_spec(B_DK, o_k),
                    col_spec(dv, o_v), col_spec(dv, o_g),
                    pl.BlockSpec((rows, 2 * GLA_RANK), lambda s, h: (tb0 + s, 0)),
                    pl.BlockSpec((2, 2 * GLA_RANK, hb * B_DK), lambda s, h: (0, 0, h)),
                    pl.BlockSpec((2, hb * B_DK), lambda s, h: (0, h)),
                    pl.BlockSpec((1, dv), lambda s, h: (0, 0))]
        args = [proj] * 2 + [proj16] * 2 + [lowrank, small["wa"], small["ba"], small["gain"]]
        out_col0 = A_VDIM
        decay_scratch = [pltpu.VMEM((rows, hb * B_DK), F32)] * 2
    has_s0 = s0 is not None
    if has_s0:
        in_specs.append(pl.BlockSpec((seq_blk, 1, 2, hb, A_DK, dv), lambda s, h: (s, e, 0, h, 0, 0)))
        args.append(s0)
    in_specs.append(pl.BlockSpec(memory_space=pl.ANY))
    aliases = {len(args): 0}
    args.append(prev)
    out_blk0 = out_col0 // (hb * dv)
    merged, states = pl.pallas_call(
        functools.partial(_gla_kernel, variant=variant, layer=layer, n_seq=seq_blk, seq_len=t, hb=hb, dv=dv,
                          has_s0=has_s0),
        grid=(n_seq // seq_blk, heads // hb),
        in_specs=in_specs,
        out_specs=[pl.BlockSpec((rows, hb * dv), lambda s, h: (tb0 + s, out_blk0 + h)),
                   pl.BlockSpec((seq_blk, 1, 2, hb, A_DK, dv), lambda s, h: (s, 0, 0, h, 0, 0))],
        out_shape=[jax.ShapeDtypeStruct((TOKENS, A_VDIM + B_VDIM), BF16),
                   jax.ShapeDtypeStruct((n_seq, 1, 2, heads, A_DK, dv), F32)],
        scratch_shapes=[pltpu.VMEM((rows, hb * dv), F32), pltpu.VMEM((seq_blk * hb * 2, A_DK, dv), F32)]
                       + decay_scratch,
        input_output_aliases=aliases,
        compiler_params=_params("parallel", "arbitrary"),
        name=f"{variant}_scan",
    )(*args)
    return merged, states


def _affine_scan_in_tiles(a, u, sub, reverse):
    s = 1
    while s < SUBLANES:
        if reverse:
            ok, shift = sub < SUBLANES - s, SUBLANES - s
        else:
            ok, shift = sub >= s, s
        u = u + a * jnp.where(ok, pltpu.roll(u, shift, 1), 0.0)
        a = a * jnp.where(ok, pltpu.roll(a, shift, 1), 1.0)
        s *= 2
    return a, u


def _lru_kernel(*refs, seq_len, row_len, has_s0):
    refs = list(refs)
    xr_ref, gate_ref, cw_ref, cb_ref, rgw_ref, rgb_ref, igw_ref, igb_ref, lam_ref = refs[:9]
    refs = refs[9:]
    s0_ref = refs.pop(0) if has_s0 else None
    refs.pop(0)
    out_ref, st_ref = refs[:2]
    xc_scr, y_scr, a_scr, u_scr = refs[2:]

    t = seq_len
    w = LRU_BW
    tl = TIME_TILE
    n_tiles = t // tl
    n_groups = tl // SUBLANES
    pos = lax.broadcasted_iota(jnp.int32, (tl, w), 0) % row_len
    sub = lax.broadcasted_iota(jnp.int32, (n_groups, SUBLANES, w), 1)
    sub2 = lax.broadcasted_iota(jnp.int32, (n_groups // SUBLANES, SUBLANES, w), 1)
    gpos = lax.broadcasted_iota(jnp.int32, (n_groups, w), 0)
    spread = jnp.where(lax.broadcasted_iota(jnp.int32, (tl, 4 * n_groups), 0) // SUBLANES
                       == lax.broadcasted_iota(jnp.int32, (tl, 4 * n_groups), 1) % n_groups, 1.0, 0.0).astype(BF16)
    cw = cw_ref[...]

    def conv(x):
        xc = cb_ref[...] + jnp.where(pos >= 2, pltpu.roll(x, 2, 0), 0.0) * cw[0:1, :]
        xc = xc + jnp.where(pos >= 1, pltpu.roll(x, 1, 0), 0.0) * cw[1:2, :]
        xc = xc + x * cw[2:3, :]
        return xc + jnp.where(pos < row_len - 1, pltpu.roll(x, tl - 1, 0), 0.0) * cw[3:4, :]

    def tile_pass(d):
        nsp = -LRU_C * _softplus(-lam_ref[d:d + 1, :])
        edge = SUBLANES - 1 if d == 0 else 0

        def tile_body(i, hprev):
            tile = i if d == 0 else n_tiles - 1 - i
            r0 = pl.multiple_of(tile * tl, tl)
            rows = pl.ds(r0, tl)
            if d == 0:
                xc = conv(xr_ref[rows, :])
                xc_scr[rows, :] = xc
            else:
                xc = xc_scr[rows, :]
            xcb = xc.astype(BF16)
            r = _sigmoid(jnp.dot(xcb, rgw_ref[d, 0].astype(BF16), preferred_element_type=F32)
                         + rgb_ref[d:d + 1, :])
            ig = _sigmoid(jnp.dot(xcb, igw_ref[d, 0].astype(BF16), preferred_element_type=F32)
                          + igb_ref[d:d + 1, :])
            log_a = r * nsp
            a = jnp.exp(log_a)
            u = jnp.sqrt(-jnp.tanh(log_a) * (a * a + 1.0)) * (ig * xc)
            a, u = _affine_scan_in_tiles(a.reshape(n_groups, SUBLANES, w), u.reshape(n_groups, SUBLANES, w),
                                         sub, d == 1)
            a = a.reshape(tl, w)
            u = u.reshape(tl, w)
            n2 = n_groups // SUBLANES
            edge_rows = pl.ds(edge, n_groups, stride=SUBLANES)
            for lt in range(w // LANES):
                a_scr[lt] = a[:, lt * LANES:(lt + 1) * LANES]
                u_scr[lt] = u[:, lt * LANES:(lt + 1) * LANES]
            ae = jnp.concatenate([a_scr[lt, edge_rows, :] for lt in range(w // LANES)], axis=1)
            ue = jnp.concatenate([u_scr[lt, edge_rows, :] for lt in range(w // LANES)], axis=1)
            a2, u2 = _affine_scan_in_tiles(ae.reshape(n2, SUBLANES, w), ue.reshape(n2, SUBLANES, w),
                                           sub2, d == 1)
            pa, pu = [None] * n2, [None] * n2
            ca = cu = None
            for j in (range(n2 - 1, -1, -1) if d == 1 else range(n2)):
                aj, uj = a2[j], u2[j]
                if ca is not None:
                    uj = uj + aj * cu
                    aj = aj * ca
                pa[j], pu[j] = aj, uj
                ca, cu = aj[edge:edge + 1, :], uj[edge:edge + 1, :]
            c = jnp.concatenate(pu, axis=0) + jnp.concatenate(pa, axis=0) * hprev
            if d == 0:
                cin = jnp.where(gpos >= 1, pltpu.roll(c, 1, 0), hprev)
            else:
                cin = jnp.where(gpos < n_groups - 1, pltpu.roll(c, n_groups - 1, 0), hprev)
            c1 = cin.astype(BF16)
            r1 = cin - c1.astype(F32)
            c2 = r1.astype(BF16)
            c3 = (r1 - c2.astype(F32)).astype(BF16)
            cin_rows = jnp.dot(spread, jnp.concatenate([c1, c2, c3, jnp.zeros_like(c3)], axis=0),
                               preferred_element_type=F32)
            hs = u + a * cin_rows
            if d == 0:
                y_scr[rows, :] = hs
            else:
                y = y_scr[rows, :] + hs
                out_ref[rows, :] = (y * _gelu_tanh(gate_ref[rows, :].astype(F32))).astype(out_ref.dtype)
            return c[n_groups - 1:n_groups, :] if d == 0 else c[0:1, :]

        h0 = s0_ref[0, 0, d:d + 1, :] if has_s0 else jnp.zeros((1, w), F32)
        st_ref[0, 0, d:d + 1, :] = lax.fori_loop(0, n_tiles, tile_body, h0)

    tile_pass(0)
    tile_pass(1)


def _lru_scan(e, proj, proj16, prev, s0, n_seq, seq_len, row_len, row0, p):
    t = seq_len
    tb0 = row0 // t
    has_s0 = s0 is not None
    in_specs = [pl.BlockSpec((t, LRU_BW), lambda s, n: (tb0 + s, n)),
                pl.BlockSpec((t, LRU_BW), lambda s, n: (tb0 + s, n)),
                pl.BlockSpec((CONV_W, LRU_BW), lambda s, n: (0, n)),
                pl.BlockSpec((1, LRU_BW), lambda s, n: (0, n)),
                pl.BlockSpec((2, 1, LRU_BW, LRU_BW), lambda s, n: (0, n, 0, 0)),
                pl.BlockSpec((2, LRU_BW), lambda s, n: (0, n)),
                pl.BlockSpec((2, 1, LRU_BW, LRU_BW), lambda s, n: (0, n, 0, 0)),
                pl.BlockSpec((2, LRU_BW), lambda s, n: (0, n)),
                pl.BlockSpec((2, LRU_BW), lambda s, n: (0, n))]
    args = [proj, proj16, p["conv_w"], p["conv_b"], p["rg_w"], p["rg_b"], p["ig_w"], p["ig_b"], p["lam"]]
    if has_s0:
        in_specs.append(pl.BlockSpec((1, 1, 2, LRU_BW), lambda s, n: (s, e, 0, n)))
        args.append(s0)
    in_specs.append(pl.BlockSpec(memory_space=pl.ANY))
    aliases = {len(args): 0}
    args.append(prev)
    return pl.pallas_call(
        functools.partial(_lru_kernel, seq_len=t, row_len=row_len, has_s0=has_s0),
        grid=(n_seq, LRU_BLOCKS),
        in_specs=in_specs,
        out_specs=[pl.BlockSpec((t, LRU_BW), lambda s, n: (tb0 + s, n)),
                   pl.BlockSpec((1, 1, 2, LRU_BW), lambda s, n: (s, 0, 0, n))],
        out_shape=[jax.ShapeDtypeStruct((TOKENS, LRU_WIDTH), BF16),
                   jax.ShapeDtypeStruct((n_seq, 1, 2, LRU_WIDTH), F32)],
        scratch_shapes=[pltpu.VMEM((t, LRU_BW), F32), pltpu.VMEM((t, LRU_BW), F32),
                        pltpu.VMEM((LRU_BW // LANES, TIME_TILE, LANES), F32),
                        pltpu.VMEM((LRU_BW // LANES, TIME_TILE, LANES), F32)],
        input_output_aliases=aliases,
        compiler_params=_params("parallel", "arbitrary"),
        name="lru_scan",
    )(*args)


def kernel(x_prompt, x_sample, state_hgrn, state_gla, state_rglru, c, c_ctx, norm1_g, norm2_g, w_mod, b_mod,
           ffn_w_in, ffn_w_out, hgrn_lower_bounds, even_w_in, gla_w_alpha, gla_b_alpha, hgrn_norm_g,
           gla_norm_g, even_w_out, odd_w_in, conv_w, conv_b, rg_w, rg_b, ig_w, ig_b, lru_lambda, odd_w_out,
           final_norm_g):
    c3 = jnp.concatenate([c_ctx[None, :], c, jnp.zeros((MOD_ROWS - 1 - DEC_BATCH, D_MODEL), F32)], axis=0)
    mod = _modulation(c3, w_mod, b_mod)
    x, h = _stack_tokens(x_prompt.reshape(N_PROMPT, D_MODEL), x_sample.reshape(N_SAMPLE, D_MODEL),
                         norm1_g[0][None, :], mod)

    new_hgrn, new_gla, new_lru = [], [], []
    y_prompt = y_sample = None
    for l in range(DEPTH):
        e = l // 2
        if l % 2 == 0:
            w16, w16_low = _feature_major_to_bf16(jnp.swapaxes(even_w_in, 1, 2), e, EVEN_MAIN, 2 * GLA_RANK)
            nb = A_KDIM // INPROJ_TN
            proj, lowrank = _inproj(h, w16, 0, [(0, 3 * nb), (5 * nb, nb)], F32, w16_low)
            proj16 = _inproj(h, w16, 0, [(3 * nb, 2 * nb), (6 * nb, 2 * nb)], BF16)
            wa = jnp.zeros((2, 2 * GLA_RANK, B_KDIM), F32)
            wa = wa.at[0, :GLA_RANK].set(gla_w_alpha[e, 0]).at[1, GLA_RANK:].set(gla_w_alpha[e, 1])
            small_a = {"lb": hgrn_lower_bounds, "gain": hgrn_norm_g[e][None, :]}
            small_b = {"wa": wa, "ba": gla_b_alpha[e], "gain": gla_norm_g[e][None, :]}
            merged = jnp.zeros((TOKENS, A_VDIM + B_VDIM), BF16)
            merged, sa = _gla_scan("hgrn", l, e, proj, proj16, None, merged, None, BATCH, SEQ, 0, small_a, 4, 1)
            merged, sb = _gla_scan("gla", l, e, proj, proj16, lowrank, merged, None, BATCH, SEQ, 0, small_b, 4, 1)
            merged, _ = _gla_scan("hgrn", l, e, proj, proj16, None, merged, state_hgrn, DEC_BATCH, DEC_SEQ, N_PROMPT,
                                  small_a, 2, 2)
            merged, _ = _gla_scan("gla", l, e, proj, proj16, lowrank, merged, state_gla, DEC_BATCH, DEC_SEQ, N_PROMPT,
                                  small_b, 2, 1)
            new_hgrn.append(sa)
            new_gla.append(sb)
            w_out = even_w_out
        else:
            nb = LRU_WIDTH // INPROJ_TN
            proj = _inproj(h, odd_w_in, e, [(0, nb)], F32)
            proj16 = _inproj(h, odd_w_in, e, [(nb, nb)], BF16)
            p = {"conv_w": conv_w[e], "conv_b": conv_b[e][None, :], "rg_w": rg_w[e], "rg_b": rg_b[e],
                 "ig_w": ig_w[e], "ig_b": ig_b[e], "lam": lru_lambda[e]}
            merged = jnp.zeros((TOKENS, LRU_WIDTH), BF16)
            merged, sr = _lru_scan(e, proj, proj16, merged, None, BATCH, SEQ, SEQ, 0, p)
            merged, _ = _lru_scan(e, proj, proj16, merged, state_rglru, DEC_BATCH, DEC_SEQ, GRID_W, N_PROMPT, p)
            new_lru.append(sr)
            w_out = odd_w_out
        x, h = _outproj(merged, w_out, e, x, mod, l, norm2_g[l][None, :])
        hidden = _ffn_up(h, l, ffn_w_in)
        x = _ffn_down(hidden, ffn_w_out, x, mod, l)
        if l + 1 < DEPTH:
            h = _norm_mod_rows(x, norm1_g[l + 1][None, :], mod, l + 1)
    gf = final_norm_g[None, :]
    y_prompt = _final_norm(x, gf, 0, N_PROMPT)
    y_sample = _final_norm(x, gf, N_PROMPT, N_SAMPLE)
    dt = x_prompt.dtype
    return (y_prompt.reshape(BATCH, SEQ, D_MODEL), y_sample.reshape(DEC_BATCH, DEC_SEQ, D_MODEL),
            jnp.concatenate(new_hgrn, axis=1).astype(dt), jnp.concatenate(new_gla, axis=1).astype(dt),
            jnp.concatenate(new_lru, axis=1).astype(dt))
```

```python
import functools

import jax
import jax.numpy as jnp
from jax import lax
from jax.experimental import pallas as pl
from jax.experimental.pallas import tpu as pltpu

F32 = jnp.float32
BF16 = jnp.bfloat16

D_MODEL = 2048
BATCH = 16
SEQ = 256
DEPTH = 2
DEC_BATCH = 2
DEC_SEQ = 2048
GRID_W = 64
N_EVEN = (DEPTH + 1) // 2
N_ODD = DEPTH // 2
A_HEADS = 8
A_DK = 128
A_DV = 128
A_KDIM = A_HEADS * A_DK
A_VDIM = A_HEADS * A_DV
B_HEADS = 4
B_DK = 128
B_DV = 256
B_KDIM = B_HEADS * B_DK
B_VDIM = B_HEADS * B_DV
GLA_RANK = 16
GLA_GATE_NORM = 16.0
SCAN_CHUNK = 64
EVEN_MAIN = 3 * A_KDIM + 2 * A_VDIM + 2 * B_KDIM + 2 * B_VDIM
LRU_WIDTH = D_MODEL
LRU_BLOCKS = 8
LRU_BW = LRU_WIDTH // LRU_BLOCKS
LRU_C = 8.0
CONV_W = 4
D_FF = (8 * D_MODEL + 3 * 256 - 1) // (3 * 256) * 256
EPS = 1e-6
LOG2_E = 1.4426950408889634

N_PROMPT = BATCH * SEQ
N_SAMPLE = DEC_BATCH * DEC_SEQ
TOKENS = N_PROMPT + N_SAMPLE
MOD_ROWS = 8
SUBLANES = 8
LANES = 128
TIME_TILE = 256
VMEM_LIMIT = 56 * 1024 * 1024


def _sigmoid(x):
    return 0.5 * (jnp.tanh(0.5 * x) + 1.0)


def _silu(x):
    return x * _sigmoid(x)


def _softplus(x):
    return jnp.maximum(x, 0.0) + jnp.log1p(jnp.exp(-jnp.abs(x)))


def _gelu_tanh(x):
    c = 0.7978845608028654
    return x * (0.5 * (1.0 + jnp.tanh(c * (x + 0.044715 * (x * x * x)))))


def _mod_row(tile, tile_rows):
    r0 = tile * tile_rows
    return jnp.where(r0 < N_PROMPT, 0, 1 + (r0 - N_PROMPT) // DEC_SEQ)


def _mod_spec(layer, which, tile_rows, tile0=0, width=D_MODEL):
    base = (layer * 6 + which) * MOD_ROWS
    if width == D_MODEL:
        return pl.BlockSpec((1, 1, width), lambda *ids: (base + _mod_row(tile0 + ids[0], tile_rows), 0, 0))
    return pl.BlockSpec((1, 1, width), lambda i, j: (base + _mod_row(tile0 + i, tile_rows), 0, j))


def _row_spec():
    return pl.BlockSpec((1, D_MODEL), lambda *ids: (0, 0))


def _norm_mod(x, g, sc, sh):
    ms = jnp.mean(x * x, axis=-1, keepdims=True)
    y = (x * lax.rsqrt(ms + EPS)) * g
    return (y * (1.0 + sc) + sh).astype(BF16)


def _params(*sem):
    return pltpu.CompilerParams(dimension_semantics=sem, vmem_limit_bytes=VMEM_LIMIT)


def _stack_kernel(p_ref, s_ref, g_ref, sc_ref, sh_ref, o_ref, h_ref, *, prompt_tiles):
    def emit(x):
        o_ref[...] = x
        h_ref[...] = _norm_mod(x, g_ref[...], sc_ref[0], sh_ref[0])

    @pl.when(pl.program_id(0) < prompt_tiles)
    def _():
        emit(p_ref[...])

    @pl.when(pl.program_id(0) >= prompt_tiles)
    def _():
        emit(s_ref[...])


def _stack_tokens(xp, xs, g, mod):
    tm = 512
    pt = N_PROMPT // tm
    return pl.pallas_call(
        functools.partial(_stack_kernel, prompt_tiles=pt), grid=(TOKENS // tm,),
        in_specs=[pl.BlockSpec((tm, D_MODEL), lambda i: (jnp.minimum(i, pt - 1), 0)),
                  pl.BlockSpec((tm, D_MODEL), lambda i: (jnp.maximum(i - pt, 0), 0)),
                  _row_spec(), _mod_spec(0, 1, tm), _mod_spec(0, 0, tm)],
        out_specs=[pl.BlockSpec((tm, D_MODEL), lambda i: (i, 0)),
                   pl.BlockSpec((tm, D_MODEL), lambda i: (i, 0))],
        out_shape=[jax.ShapeDtypeStruct((TOKENS, D_MODEL), F32),
                   jax.ShapeDtypeStruct((TOKENS, D_MODEL), BF16)],
        compiler_params=_params("arbitrary"),
        name="stack_tokens",
    )(xp, xs, g, mod, mod)


def _mod_kernel(c_ref, w_ref, b_ref, o_ref):
    s = _silu(c_ref[...]).astype(BF16)
    o_ref[0, 0] = jnp.dot(s, w_ref[0].astype(BF16), preferred_element_type=F32) + b_ref[0]


def _modulation(c3, w_mod, b_mod):
    tn = 1024
    nb = D_MODEL // tn
    out = pl.pallas_call(
        _mod_kernel,
        grid=(DEPTH, 6, nb),
        in_specs=[pl.BlockSpec((MOD_ROWS, D_MODEL), lambda l, k, j: (0, 0)),
                  pl.BlockSpec((1, D_MODEL, tn), lambda l, k, j: (l, 0, k * nb + j)),
                  pl.BlockSpec((1, 1, tn), lambda l, k, j: (l, 0, k * nb + j))],
        out_specs=pl.BlockSpec((1, 1, MOD_ROWS, tn), lambda l, k, j: (l, k, 0, j)),
        out_shape=jax.ShapeDtypeStruct((DEPTH, 6, MOD_ROWS, D_MODEL), F32),
        compiler_params=_params("arbitrary", "arbitrary", "arbitrary"),
        name="modulation",
    )(c3, w_mod, b_mod.reshape(DEPTH, 1, 6 * D_MODEL))
    return out.reshape(DEPTH * 6 * MOD_ROWS, 1, D_MODEL)


INPROJ_TN = 512


def _relayout_kernel(wt_ref, wlt_ref, o_ref, ol_ref):
    o_ref[...] = wt_ref[...].T.astype(BF16)

    @pl.when(pl.program_id(0) == 0)
    def _():
        ol_ref[...] = wlt_ref[...].T.astype(BF16)


def _feature_major_to_bf16(wt, widx, n_main, n_low):
    tn = INPROJ_TN
    return pl.pallas_call(
        _relayout_kernel, grid=(n_main // tn,),
        in_specs=[pl.BlockSpec((None, tn, D_MODEL), lambda j: (widx, j, 0)),
                  pl.BlockSpec((None, n_low, D_MODEL), lambda j: (widx, n_main // n_low, 0))],
        out_specs=[pl.BlockSpec((None, D_MODEL, tn), lambda j: (0, 0, j)),
                   pl.BlockSpec((D_MODEL, n_low), lambda j: (0, 0))],
        out_shape=[jax.ShapeDtypeStruct((1, D_MODEL, n_main), BF16),
                   jax.ShapeDtypeStruct((D_MODEL, n_low), BF16)],
        compiler_params=_params("arbitrary"),
        name="weight_relayout",
    )(wt, wt)


def _inproj_kernel(h_ref, w_ref, o_ref):
    o_ref[...] = jnp.dot(h_ref[...], w_ref[...].astype(BF16), preferred_element_type=F32).astype(o_ref.dtype)


def _inproj_lowrank_kernel(h_ref, w_ref, wl_ref, o_ref, ol_ref):
    @pl.when(pl.program_id(1) == 0)
    def _():
        ol_ref[...] = jnp.dot(h_ref[...], wl_ref[...].astype(BF16), preferred_element_type=F32)

    o_ref[...] = jnp.dot(h_ref[...], w_ref[...].astype(BF16), preferred_element_type=F32).astype(o_ref.dtype)


def _inproj(h, w, widx, groups, out_dtype, w_low=None):
    tm, tn = 2048, INPROJ_TN
    (b0, n0), (b1, _) = (groups + [(0, 0)])[:2]
    n_blocks = sum(n for _, n in groups)

    def wblock(j):
        return jnp.where(j < n0, b0 + j, b1 + j - n0) if len(groups) > 1 else b0 + j

    in_specs = [pl.BlockSpec((tm, D_MODEL), lambda i, j: (i, 0)),
                pl.BlockSpec((None, D_MODEL, tn), lambda i, j: (widx, 0, wblock(j)))]
    out_specs = pl.BlockSpec((tm, tn), lambda i, j: (i, j))
    out_shape = jax.ShapeDtypeStruct((TOKENS, n_blocks * tn), out_dtype)
    args = [h, w]
    body = _inproj_kernel
    if w_low is not None:
        nl = w_low.shape[1]
        in_specs.append(pl.BlockSpec((D_MODEL, nl), lambda i, j: (0, 0)))
        out_specs = [out_specs, pl.BlockSpec((tm, nl), lambda i, j: (i, 0))]
        out_shape = [out_shape, jax.ShapeDtypeStruct((TOKENS, nl), F32)]
        args.append(w_low)
        body = _inproj_lowrank_kernel
    return pl.pallas_call(
        body, grid=(TOKENS // tm, n_blocks), in_specs=in_specs, out_specs=out_specs, out_shape=out_shape,
        compiler_params=_params("parallel", "arbitrary"),
        name="inproj",
    )(*args)


def _outproj_kernel(a_ref, w_ref, x_ref, gate_ref, g_ref, sc_ref, sh_ref, o_ref, h_ref, w_scr):
    @pl.when(pl.program_id(0) == 0)
    def _():
        w_scr[...] = w_ref[...].astype(BF16)

    acc = jnp.dot(a_ref[...], w_scr[...], preferred_element_type=F32)
    y = x_ref[...] + gate_ref[0] * acc
    o_ref[...] = y
    h_ref[...] = _norm_mod(y, g_ref[...], sc_ref[0], sh_ref[0])


def _outproj(a, w, widx, x, mod, layer, g2):
    tm = 512
    k = a.shape[1]
    return pl.pallas_call(
        _outproj_kernel, grid=(TOKENS // tm,),
        in_specs=[pl.BlockSpec((tm, k), lambda i: (i, 0)),
                  pl.BlockSpec((None, k, D_MODEL), lambda i: (widx, 0, 0), pipeline_mode=pl.Buffered(1)),
                  pl.BlockSpec((tm, D_MODEL), lambda i: (i, 0)),
                  _mod_spec(layer, 2, tm), _row_spec(), _mod_spec(layer, 4, tm), _mod_spec(layer, 3, tm)],
        out_specs=[pl.BlockSpec((tm, D_MODEL), lambda i: (i, 0)),
                   pl.BlockSpec((tm, D_MODEL), lambda i: (i, 0))],
        out_shape=[jax.ShapeDtypeStruct((TOKENS, D_MODEL), F32),
                   jax.ShapeDtypeStruct((TOKENS, D_MODEL), BF16)],
        scratch_shapes=[pltpu.VMEM((k, D_MODEL), BF16)],
        input_output_aliases={2: 0},
        compiler_params=_params("arbitrary"),
        name="outproj",
    )(a, w, x, mod, g2, mod, mod)


def _ffn_up_kernel(h_ref, wg_ref, wu_ref, o_ref):
    h = h_ref[...]
    gt = jnp.dot(h, wg_ref[...].astype(BF16), preferred_element_type=F32)
    up = jnp.dot(h, wu_ref[...].astype(BF16), preferred_element_type=F32)
    o_ref[...] = (_silu(gt) * up).astype(BF16)


def _ffn_up(h, layer, w_in):
    tm, tf = 2048, 512
    nf = D_FF // tf
    return pl.pallas_call(
        _ffn_up_kernel, grid=(TOKENS // tm, nf),
        in_specs=[pl.BlockSpec((tm, D_MODEL), lambda i, j: (i, 0)),
                  pl.BlockSpec((None, D_MODEL, tf), lambda i, j: (layer, 0, j)),
                  pl.BlockSpec((None, D_MODEL, tf), lambda i, j: (layer, 0, nf + j))],
        out_specs=pl.BlockSpec((tm, tf), lambda i, j: (i, j)),
        out_shape=jax.ShapeDtypeStruct((TOKENS, D_FF), BF16),
        compiler_params=_params("parallel", "arbitrary"),
        name="ffn_up",
    )(h, w_in, w_in)


def _ffn_down_kernel(a_ref, w_ref, x_ref, gate_ref, o_ref, w_scr):
    @pl.when(pl.program_id(1) == 0)
    def _():
        w_scr[...] = w_ref[...].astype(BF16)

    acc = jnp.dot(a_ref[...], w_scr[...], preferred_element_type=F32)
    o_ref[...] = x_ref[...] + gate_ref[0] * acc


def _ffn_down(a, w, x, mod, layer):
    tm, tn = 512, 512
    base = (layer * 6 + 5) * MOD_ROWS
    return pl.pallas_call(
        _ffn_down_kernel, grid=(D_MODEL // tn, TOKENS // tm),
        in_specs=[pl.BlockSpec((tm, D_FF), lambda j, i: (i, 0)),
                  pl.BlockSpec((None, D_FF, tn), lambda j, i: (layer, 0, j)),
                  pl.BlockSpec((tm, tn), lambda j, i: (i, j)),
                  pl.BlockSpec((1, 1, tn), lambda j, i: (base + _mod_row(i, tm), 0, j))],
        out_specs=pl.BlockSpec((tm, tn), lambda j, i: (i, j)),
        out_shape=jax.ShapeDtypeStruct((TOKENS, D_MODEL), F32),
        scratch_shapes=[pltpu.VMEM((D_FF, tn), BF16)],
        input_output_aliases={2: 0},
        compiler_params=_params("arbitrary", "arbitrary"),
        name="ffn_down",
    )(a, w, x, mod)


def _norm_mod_kernel(x_ref, g_ref, sc_ref, sh_ref, h_ref):
    h_ref[...] = _norm_mod(x_ref[...], g_ref[...], sc_ref[0], sh_ref[0])


def _norm_mod_rows(x, g, mod, layer):
    tm = 512
    return pl.pallas_call(
        _norm_mod_kernel, grid=(TOKENS // tm,),
        in_specs=[pl.BlockSpec((tm, D_MODEL), lambda i: (i, 0)),
                  _row_spec(), _mod_spec(layer, 1, tm), _mod_spec(layer, 0, tm)],
        out_specs=pl.BlockSpec((tm, D_MODEL), lambda i: (i, 0)),
        out_shape=jax.ShapeDtypeStruct((TOKENS, D_MODEL), BF16),
        compiler_params=_params("parallel"),
        name="norm_mod",
    )(x, g, mod, mod)


def _final_norm_kernel(x_ref, g_ref, o_ref):
    x = x_ref[...]
    ms = jnp.mean(x * x, axis=-1, keepdims=True)
    o_ref[...] = (x * lax.rsqrt(ms + EPS)) * g_ref[...]


def _final_norm(x, g, row0, rows):
    tm = 512
    t0 = row0 // tm
    return pl.pallas_call(
        _final_norm_kernel, grid=(rows // tm,),
        in_specs=[pl.BlockSpec((tm, D_MODEL), lambda i: (t0 + i, 0)), _row_spec()],
        out_specs=pl.BlockSpec((tm, D_MODEL), lambda i: (i, 0)),
        out_shape=jax.ShapeDtypeStruct((rows, D_MODEL), F32),
        compiler_params=_params("parallel"),
        name="final_norm",
    )(x, g)


def _chunk_cumsum(g, sub, reverse):
    rows, lanes = g.shape
    tiles = rows // SUBLANES
    x = g.reshape(tiles, SUBLANES, lanes)
    s = 1
    while s < SUBLANES:
        if reverse:
            x = x + jnp.where(sub < SUBLANES - s, pltpu.roll(x, SUBLANES - s, 1), 0.0)
        else:
            x = x + jnp.where(sub >= s, pltpu.roll(x, s, 1), 0.0)
        s *= 2
    parts = [None] * tiles
    carry = None
    for j in (range(tiles - 1, -1, -1) if reverse else range(tiles)):
        xj = x[j] if carry is None else x[j] + carry
        parts[j] = xj
        carry = xj[0:1, :] if reverse else xj[SUBLANES - 1:SUBLANES, :]
    return jnp.concatenate(parts, axis=0)


def _gla_kernel(*refs, variant, layer, n_seq, seq_len, hb, dv, has_s0):
    refs = list(refs)
    if variant == "hgrn":
        q_ref, ff_ref, fb_ref, v_ref, gate_ref, lb_ref, gain_ref = refs[:7]
        refs = refs[7:]
        gate_in = (ff_ref, fb_ref)
    else:
        q_ref, k_ref, v_ref, gate_ref, lr_ref, wa_ref, ba_ref, gain_ref = refs[:8]
        refs = refs[8:]
    s0_ref = refs.pop(0) if has_s0 else None
    refs.pop(0)
    out_ref, st_ref = refs[:2]
    o_scr, s_scr = refs[2:4]
    g_scr = refs[4:]

    t = seq_len
    n = t // SCAN_CHUNK
    half = SCAN_CHUNK // 2
    sub = lax.broadcasted_iota(jnp.int32, (SCAN_CHUNK // SUBLANES, SUBLANES, A_DK), 1)
    ri = lax.broadcasted_iota(jnp.int32, (SCAN_CHUNK, SCAN_CHUNK), 0)
    ci = lax.broadcasted_iota(jnp.int32, (SCAN_CHUNK, SCAN_CHUNK), 1)
    masks = (ci <= ri, ci >= ri)
    nt_dims = (((1,), (1,)), ((), ()))
    tn_dims = (((0,), (0,)), ((), ()))
    gain = gain_ref[...]

    rows_total = n_seq * t
    if variant == "hgrn":
        lbr = lb_ref[...]
        ex = jnp.exp(lbr - jnp.max(lbr, axis=0, keepdims=True))
        lb = jnp.sum(ex[:layer + 1], axis=0, keepdims=True) / jnp.sum(ex, axis=0, keepdims=True)
    else:
        def decay_tile(i, carry):
            rows = pl.ds(pl.multiple_of(i * TIME_TILE, TIME_TILE), TIME_TILE)
            lr = lr_ref[rows, :].astype(BF16)
            for d in range(2):
                z = jnp.dot(lr, wa_ref[d].astype(BF16), preferred_element_type=F32) + ba_ref[d:d + 1, :]
                g_scr[d][rows, :] = -_softplus(-z) * (LOG2_E / GLA_GATE_NORM)
            return carry

        lax.fori_loop(0, rows_total // TIME_TILE, decay_tile, 0)

    def chunk_step(c, s, h, d):
        rows = pl.ds(pl.multiple_of(s * t + c * SCAN_CHUNK, SCAN_CHUNK), SCAN_CHUNK)
        kc = slice(h * A_DK, (h + 1) * A_DK)
        vc = slice(h * dv, (h + 1) * dv)
        chain = (s * hb + h) * 2 + d
        q = q_ref[rows, kc] * (A_DK ** -0.5)
        if variant == "hgrn":
            lbh = lb[:, kc]
            f = lbh + (1.0 - lbh) * _sigmoid(gate_in[d][rows, kc])
            k = 1.0 - f
            g = jnp.log(f) * LOG2_E
        else:
            k = k_ref[rows, kc]
            g = g_scr[d][rows, kc]
        b = _chunk_cumsum(g, sub, d == 1)
        tot = b[0:1, :] if d == 1 else b[SCAN_CHUNK - 1:SCAN_CHUNK, :]
        mid = b[half:half + 1, :] if d == 1 else b[half - 1:half, :]
        qs = (q * jnp.exp2(b)).astype(BF16)
        qa = (q * jnp.exp2(b - mid)).astype(BF16)
        ka = (k * jnp.exp2(mid - b)).astype(BF16)
        kt = (k * jnp.exp2(tot - b)).astype(BF16)
        vb = v_ref[rows, vc].astype(BF16)
        s_t = s_scr[chain]
        att = lax.dot_general(qa, ka, nt_dims, preferred_element_type=F32)
        att = jnp.where(masks[d], att, 0.0).astype(BF16)
        o = jnp.dot(jnp.concatenate([qs, att], axis=1),
                    jnp.concatenate([s_t.astype(BF16), vb], axis=0), preferred_element_type=F32)
        decay = jnp.broadcast_to(jnp.exp2(tot), (A_DK, A_DK)).T
        if dv != A_DK:
            decay = jnp.concatenate([decay] * (dv // A_DK), axis=1)
        s_new = decay * s_t + lax.dot_general(kt, vb, tn_dims, preferred_element_type=F32)
        return chain, rows, vc, o, s_new

    def finish(rows, vc, o):
        o = o + o_scr[rows, vc]
        ms = jnp.mean(o * o, axis=-1, keepdims=True)
        o = (o * lax.rsqrt(ms + EPS)) * gain
        return (o * _silu(gate_ref[rows, vc].astype(F32))).astype(out_ref.dtype)

    chains = [(s, h, d) for d in range(2) for s in range(n_seq) for h in range(hb)]

    def first_half(i, carry):
        done = [chunk_step(i if d == 0 else n - 1 - i, s, h, d) for s, h, d in chains]
        for chain, rows, vc, o, s_new in done:
            s_scr[chain] = s_new
            o_scr[rows, vc] = o + 0.0
        return carry

    def second_half(i, carry):
        done = [chunk_step(i if d == 0 else n - 1 - i, s, h, d) for s, h, d in chains]
        outs = [finish(rows, vc, o) for _, rows, vc, o, _ in done]
        for (chain, rows, vc, _, s_new), out in zip(done, outs):
            s_scr[chain] = s_new
            out_ref[rows, vc] = out
        return carry

    for s in range(n_seq):
        for h in range(hb):
            for d in range(2):
                chain = (s * hb + h) * 2 + d
                s_scr[chain] = s0_ref[s, 0, d, h] if has_s0 else jnp.zeros((A_DK, dv), F32)
    lax.fori_loop(0, n // 2, first_half, 0)
    lax.fori_loop(n // 2, n, second_half, 0)
    for s in range(n_seq):
        for h in range(hb):
            for d in range(2):
                st_ref[s, 0, d, h] = s_scr[(s * hb + h) * 2 + d]


def _gla_scan(variant, layer, e, proj, proj16, lowrank, prev, s0, n_seq, seq_len, row0, small, seq_blk, hb):
    t = seq_len
    rows = seq_blk * t
    tb0 = row0 // rows
    heads, dv = (A_HEADS, A_DV) if variant == "hgrn" else (B_HEADS, B_DV)

    def col_spec(width, col0):
        blk0 = col0 // (hb * width)
        return pl.BlockSpec((rows, hb * width), lambda s, h: (tb0 + s, blk0 + h))

    if variant == "hgrn":
        o_q, o_ff, o_fb = 0, A_KDIM, 2 * A_KDIM
        o_v, o_g = 0, A_VDIM
        in_specs = [col_spec(A_DK, o_q), col_spec(A_DK, o_ff), col_spec(A_DK, o_fb),
                    col_spec(dv, o_v), col_spec(dv, o_g),
                    pl.BlockSpec((DEPTH + 1, hb * A_DK), lambda s, h: (0, h)),
                    pl.BlockSpec((1, dv), lambda s, h: (0, 0))]
        args = [proj] * 3 + [proj16] * 2 + [small["lb"], small["gain"]]
        out_col0 = 0
        decay_scratch = []
    else:
        o_q, o_k = 3 * A_KDIM, 3 * A_KDIM + B_KDIM
        o_v, o_g = 2 * A_VDIM, 2 * A_VDIM + B_VDIM
        in_specs = [col_spec(B_DK, o_q), col_spec(B_DK, o_k),
                    col_spec(dv, o_v), col_spec(dv, o_g),
                    pl.BlockSpec((rows, 2 * GLA_RANK), lambda s, h: (tb0 + s, 0)),
                    pl.BlockSpec((2, 2 * GLA_RANK, hb * B_DK), lambda s, h: (0, 0, h)),
                    pl.BlockSpec((2, hb * B_DK), lambda s, h: (0, h)),
                    pl.BlockSpec((1, dv), lambda s, h: (0, 0))]
        args = [proj] * 2 + [proj16] * 2 + [lowrank, small["wa"], small["ba"], small["gain"]]
        out_col0 = A_VDIM
        decay_scratch = [pltpu.VMEM((rows, hb * B_DK), F32)] * 2
    has_s0 = s0 is not None
    if has_s0:
        in_specs.append(pl.BlockSpec((seq_blk, 1, 2, hb, A_DK, dv), lambda s, h: (s, e, 0, h, 0, 0)))
        args.append(s0)
    in_specs.append(pl.BlockSpec(memory_space=pl.ANY))
    aliases = {len(args): 0}
    args.append(prev)
    out_blk0 = out_col0 // (hb * dv)
    merged, states = pl.pallas_call(
        functools.partial(_gla_kernel, variant=variant, layer=layer, n_seq=seq_blk, seq_len=t, hb=hb, dv=dv,
                          has_s0=has_s0),
        grid=(n_seq // seq_blk, heads // hb),
        in_specs=in_specs,
        out_specs=[pl.BlockSpec((rows, hb * dv), lambda s, h: (tb0 + s, out_blk0 + h)),
                   pl.BlockSpec((seq_blk, 1, 2, hb, A_DK, dv), lambda s, h: (s, 0, 0, h, 0, 0))],
        out_shape=[jax.ShapeDtypeStruct((TOKENS, A_VDIM + B_VDIM), BF16),
                   jax.ShapeDtypeStruct((n_seq, 1, 2, heads, A_DK, dv), F32)],
        scratch_shapes=[pltpu.VMEM((rows, hb * dv), F32), pltpu.VMEM((seq_blk * hb * 2, A_DK, dv), F32)]
                       + decay_scratch,
        input_output_aliases=aliases,
        compiler_params=_params("parallel", "arbitrary"),
        name=f"{variant}_scan",
    )(*args)
    return merged, states


def _affine_scan_in_tiles(a, u, sub, reverse):
    s = 1
    while s < SUBLANES:
        if reverse:
            ok, shift = sub < SUBLANES - s, SUBLANES - s
        else:
            ok, shift = sub >= s, s
        u = u + a * jnp.where(ok, pltpu.roll(u, shift, 1), 0.0)
        a = a * jnp.where(ok, pltpu.roll(a, shift, 1), 1.0)
        s *= 2
    return a, u


def _lru_kernel(*refs, seq_len, row_len, has_s0):
    refs = list(refs)
    xr_ref, gate_ref, cw_ref, cb_ref, rgw_ref, rgb_ref, igw_ref, igb_ref, lam_ref = refs[:9]
    refs = refs[9:]
    s0_ref = refs.pop(0) if has_s0 else None
    refs.pop(0)
    out_ref, st_ref = refs[:2]
    xc_scr, y_scr, in_scr, out_scr = refs[2:]

    t = seq_len
    w = LRU_BW
    tl = TIME_TILE
    n_tiles = t // tl
    n_groups = tl // SUBLANES
    lane_tiles = w // LANES
    groups_per_row = row_len // SUBLANES
    sub2 = lax.broadcasted_iota(jnp.int32, (n_groups // SUBLANES, SUBLANES, w), 1)
    gpos = lax.broadcasted_iota(jnp.int32, (n_groups, w), 0)
    row_first = gpos % groups_per_row == 0
    row_last = gpos % groups_per_row == groups_per_row - 1
    cw = cw_ref[...]

    def slabs_of(x):
        for lt in range(lane_tiles):
            in_scr[lt] = x[:, lt * LANES:(lt + 1) * LANES]
        return [jnp.concatenate([in_scr[lt, pl.ds(r, n_groups, stride=SUBLANES), :]
                                 for lt in range(lane_tiles)], axis=1) for r in range(SUBLANES)]

    def time_order(slabs):
        for r in range(SUBLANES):
            for lt in range(lane_tiles):
                out_scr[lt, pl.ds(r, n_groups, stride=SUBLANES), :] = slabs[r][:, lt * LANES:(lt + 1) * LANES]
        return jnp.concatenate([out_scr[lt] for lt in range(lane_tiles)], axis=1)

    def split(x):
        return [x[r * n_groups:(r + 1) * n_groups, :] for r in range(SUBLANES)]

    def conv(x):
        back = lambda z: jnp.where(row_first, 0.0, pltpu.roll(z, 1, 0))
        fwd = lambda z: jnp.where(row_last, 0.0, pltpu.roll(z, n_groups - 1, 0))
        ext = [back(x[SUBLANES - 2]), back(x[SUBLANES - 1])] + x + [fwd(x[0])]
        out = []
        for r in range(SUBLANES):
            xc = cb_ref[...] + ext[r] * cw[0:1, :]
            xc = xc + ext[r + 1] * cw[1:2, :]
            xc = xc + ext[r + 2] * cw[2:3, :]
            out.append(xc + ext[r + 3] * cw[3:4, :])
        return jnp.concatenate(out, axis=0)

    def tile_pass(d):
        nsp = -LRU_C * _softplus(-lam_ref[d:d + 1, :])
        edge = SUBLANES - 1 if d == 0 else 0
        order = range(SUBLANES) if d == 0 else range(SUBLANES - 1, -1, -1)

        def tile_body(i, hprev):
            tile = i if d == 0 else n_tiles - 1 - i
            r0 = pl.multiple_of(tile * tl, tl)
            rows = pl.ds(r0, tl)
            if d == 0:
                xc = conv(slabs_of(xr_ref[rows, :]))
                xc_scr[rows, :] = xc
            else:
                xc = xc_scr[rows, :]
            xcb = xc.astype(BF16)
            r = _sigmoid(jnp.dot(xcb, rgw_ref[d, 0].astype(BF16), preferred_element_type=F32)
                         + rgb_ref[d:d + 1, :])
            ig = _sigmoid(jnp.dot(xcb, igw_ref[d, 0].astype(BF16), preferred_element_type=F32)
                          + igb_ref[d:d + 1, :])
            log_a = r * nsp
            a = jnp.exp(log_a)
            u = jnp.sqrt(-jnp.tanh(log_a) * (a * a + 1.0)) * (ig * xc)
            a_s, u_s = split(a), split(u)
            acc_a, acc_u = [None] * SUBLANES, [None] * SUBLANES
            pa_ = pu_ = None
            for r in order:
                if pa_ is None:
                    acc_a[r], acc_u[r] = a_s[r], u_s[r]
                else:
                    acc_u[r] = a_s[r] * pu_ + u_s[r]
                    acc_a[r] = a_s[r] * pa_
                pa_, pu_ = acc_a[r], acc_u[r]
            n2 = n_groups // SUBLANES
            a2, u2 = _affine_scan_in_tiles(acc_a[edge].reshape(n2, SUBLANES, w),
                                           acc_u[edge].reshape(n2, SUBLANES, w), sub2, d == 1)
            pa, pu = [None] * n2, [None] * n2
            ca = cu = None
            for j in (range(n2 - 1, -1, -1) if d == 1 else range(n2)):
                aj, uj = a2[j], u2[j]
                if ca is not None:
                    uj = uj + aj * cu
                    aj = aj * ca
                pa[j], pu[j] = aj, uj
                ca, cu = aj[edge:edge + 1, :], uj[edge:edge + 1, :]
            c = jnp.concatenate(pu, axis=0) + jnp.concatenate(pa, axis=0) * hprev
            if d == 0:
                cin = jnp.where(gpos >= 1, pltpu.roll(c, 1, 0), hprev)
            else:
                cin = jnp.where(gpos < n_groups - 1, pltpu.roll(c, n_groups - 1, 0), hprev)
            hs = jnp.concatenate([acc_u[r] + acc_a[r] * cin for r in range(SUBLANES)], axis=0)
            if d == 0:
                y_scr[rows, :] = hs
            else:
                gate = jnp.concatenate(slabs_of(gate_ref[rows, :]), axis=0)
                y = (y_scr[rows, :] + hs) * _gelu_tanh(gate)
                out_ref[rows, :] = time_order(split(y)).astype(out_ref.dtype)
            return c[n_groups - 1:n_groups, :] if d == 0 else c[0:1, :]

        h0 = s0_ref[0, 0, d:d + 1, :] if has_s0 else jnp.zeros((1, w), F32)
        st_ref[0, 0, d:d + 1, :] = lax.fori_loop(0, n_tiles, tile_body, h0)

    tile_pass(0)
    tile_pass(1)


def _lru_scan(e, proj, gate_proj, prev, s0, n_seq, seq_len, row_len, row0, p):
    t = seq_len
    tb0 = row0 // t
    has_s0 = s0 is not None
    in_specs = [pl.BlockSpec((t, LRU_BW), lambda s, n: (tb0 + s, n)),
                pl.BlockSpec((t, LRU_BW), lambda s, n: (tb0 + s, n)),
                pl.BlockSpec((CONV_W, LRU_BW), lambda s, n: (0, n)),
                pl.BlockSpec((1, LRU_BW), lambda s, n: (0, n)),
                pl.BlockSpec((2, 1, LRU_BW, LRU_BW), lambda s, n: (0, n, 0, 0)),
                pl.BlockSpec((2, LRU_BW), lambda s, n: (0, n)),
                pl.BlockSpec((2, 1, LRU_BW, LRU_BW), lambda s, n: (0, n, 0, 0)),
                pl.BlockSpec((2, LRU_BW), lambda s, n: (0, n)),
                pl.BlockSpec((2, LRU_BW), lambda s, n: (0, n))]
    args = [proj, gate_proj, p["conv_w"], p["conv_b"], p["rg_w"], p["rg_b"], p["ig_w"], p["ig_b"], p["lam"]]
    if has_s0:
        in_specs.append(pl.BlockSpec((1, 1, 2, LRU_BW), lambda s, n: (s, e, 0, n)))
        args.append(s0)
    in_specs.append(pl.BlockSpec(memory_space=pl.ANY))
    aliases = {len(args): 0}
    args.append(prev)
    return pl.pallas_call(
        functools.partial(_lru_kernel, seq_len=t, row_len=row_len, has_s0=has_s0),
        grid=(n_seq, LRU_BLOCKS),
        in_specs=in_specs,
        out_specs=[pl.BlockSpec((t, LRU_BW), lambda s, n: (tb0 + s, n)),
                   pl.BlockSpec((1, 1, 2, LRU_BW), lambda s, n: (s, 0, 0, n))],
        out_shape=[jax.ShapeDtypeStruct((TOKENS, LRU_WIDTH), BF16),
                   jax.ShapeDtypeStruct((n_seq, 1, 2, LRU_WIDTH), F32)],
        scratch_shapes=[pltpu.VMEM((t, LRU_BW), F32), pltpu.VMEM((t, LRU_BW), F32),
                        pltpu.VMEM((LRU_BW // LANES, TIME_TILE, LANES), F32),
                        pltpu.VMEM((LRU_BW // LANES, TIME_TILE, LANES), F32)],
        input_output_aliases=aliases,
        compiler_params=_params("parallel", "arbitrary"),
        name="lru_scan",
    )(*args)


def kernel(x_prompt, x_sample, state_hgrn, state_gla, state_rglru, c, c_ctx, norm1_g, norm2_g, w_mod, b_mod,
           ffn_w_in, ffn_w_out, hgrn_lower_bounds, even_w_in, gla_w_alpha, gla_b_alpha, hgrn_norm_g,
           gla_norm_g, even_w_out, odd_w_in, conv_w, conv_b, rg_w, rg_b, ig_w, ig_b, lru_lambda, odd_w_out,
           final_norm_g):
    c3 = jnp.concatenate([c_ctx[None, :], c, jnp.zeros((MOD_ROWS - 1 - DEC_BATCH, D_MODEL), F32)], axis=0)
    mod = _modulation(c3, w_mod, b_mod)
    x, h = _stack_tokens(x_prompt.reshape(N_PROMPT, D_MODEL), x_sample.reshape(N_SAMPLE, D_MODEL),
                         norm1_g[0][None, :], mod)

    new_hgrn, new_gla, new_lru = [], [], []
    y_prompt = y_sample = None
    for l in range(DEPTH):
        e = l // 2
        if l % 2 == 0:
            w16, w16_low = _feature_major_to_bf16(jnp.swapaxes(even_w_in, 1, 2), e, EVEN_MAIN, 2 * GLA_RANK)
            nb = A_KDIM // INPROJ_TN
            proj, lowrank = _inproj(h, w16, 0, [(0, 3 * nb), (5 * nb, nb)], F32, w16_low)
            proj16 = _inproj(h, w16, 0, [(3 * nb, 2 * nb), (6 * nb, 2 * nb)], BF16)
            wa = jnp.zeros((2, 2 * GLA_RANK, B_KDIM), F32)
            wa = wa.at[0, :GLA_RANK].set(gla_w_alpha[e, 0]).at[1, GLA_RANK:].set(gla_w_alpha[e, 1])
            small_a = {"lb": hgrn_lower_bounds, "gain": hgrn_norm_g[e][None, :]}
            small_b = {"wa": wa, "ba": gla_b_alpha[e], "gain": gla_norm_g[e][None, :]}
            merged = jnp.zeros((TOKENS, A_VDIM + B_VDIM), BF16)
            merged, sa = _gla_scan("hgrn", l, e, proj, proj16, None, merged, None, BATCH, SEQ, 0, small_a, 4, 1)
            merged, sb = _gla_scan("gla", l, e, proj, proj16, lowrank, merged, None, BATCH, SEQ, 0, small_b, 4, 1)
            merged, _ = _gla_scan("hgrn", l, e, proj, proj16, None, merged, state_hgrn, DEC_BATCH, DEC_SEQ, N_PROMPT,
                                  small_a, 2, 2)
            merged, _ = _gla_scan("gla", l, e, proj, proj16, lowrank, merged, state_gla, DEC_BATCH, DEC_SEQ, N_PROMPT,
                                  small_b, 2, 1)
            new_hgrn.append(sa)
            new_gla.append(sb)
            w_out = even_w_out
        else:
            nb = LRU_WIDTH // INPROJ_TN
            proj = _inproj(h, odd_w_in, e, [(0, nb)], F32)
            proj16 = _inproj(h, odd_w_in, e, [(nb, nb)], F32)
            p = {"conv_w": conv_w[e], "conv_b": conv_b[e][None, :], "rg_w": rg_w[e], "rg_b": rg_b[e],
                 "ig_w": ig_w[e], "ig_b": ig_b[e], "lam": lru_lambda[e]}
            merged = jnp.zeros((TOKENS, LRU_WIDTH), BF16)
            merged, sr = _lru_scan(e, proj, proj16, merged, None, BATCH, SEQ, SEQ, 0, p)
            merged, _ = _lru_scan(e, proj, proj16, merged, state_rglru, DEC_BATCH, DEC_SEQ, GRID_W, N_PROMPT, p)
            new_lru.append(sr)
            w_out = odd_w_out
        x, h = _outproj(merged, w_out, e, x, mod, l, norm2_g[l][None, :])
        hidden = _ffn_up(h, l, ffn_w_in)
        x = _ffn_down(hidden, ffn_w_out, x, mod, l)
        if l + 1 < DEPTH:
            h = _norm_mod_rows(x, norm1_g[l + 1][None, :], mod, l + 1)
    gf = final_norm_g[None, :]
    y_prompt = _final_norm(x, gf, 0, N_PROMPT)
    y_sample = _final_norm(x, gf, N_PROMPT, N_SAMPLE)
    dt = x_prompt.dtype
    return (y_prompt.reshape(BATCH, SEQ, D_MODEL), y_sample.reshape(DEC_BATCH, DEC_SEQ, D_MODEL),
            jnp.concatenate(new_hgrn, axis=1).astype(dt), jnp.concatenate(new_gla, axis=1).astype(dt),
            jnp.concatenate(new_lru, axis=1).astype(dt))
```

```python
import functools

import jax
import jax.numpy as jnp
from jax import lax
from jax.experimental import pallas as pl
from jax.experimental.pallas import tpu as pltpu

F32 = jnp.float32
BF16 = jnp.bfloat16

D_MODEL = 2048
BATCH = 16
SEQ = 256
DEPTH = 2
DEC_BATCH = 2
DEC_SEQ = 2048
GRID_W = 64
N_EVEN = (DEPTH + 1) // 2
N_ODD = DEPTH // 2
A_HEADS = 8
A_DK = 128
A_DV = 128
A_KDIM = A_HEADS * A_DK
A_VDIM = A_HEADS * A_DV
B_HEADS = 4
B_DK = 128
B_DV = 256
B_KDIM = B_HEADS * B_DK
B_VDIM = B_HEADS * B_DV
GLA_RANK = 16
GLA_GATE_NORM = 16.0
SCAN_CHUNK = 64
EVEN_MAIN = 3 * A_KDIM + 2 * A_VDIM + 2 * B_KDIM + 2 * B_VDIM
LRU_WIDTH = D_MODEL
LRU_BLOCKS = 8
LRU_BW = LRU_WIDTH // LRU_BLOCKS
LRU_C = 8.0
CONV_W = 4
D_FF = (8 * D_MODEL + 3 * 256 - 1) // (3 * 256) * 256
EPS = 1e-6
LOG2_E = 1.4426950408889634

N_PROMPT = BATCH * SEQ
N_SAMPLE = DEC_BATCH * DEC_SEQ
TOKENS = N_PROMPT + N_SAMPLE
MOD_ROWS = 8
SUBLANES = 8
LANES = 128
TIME_TILE = 256
VMEM_LIMIT = 56 * 1024 * 1024


def _sigmoid(x):
    return 0.5 * (jnp.tanh(0.5 * x) + 1.0)


def _silu(x):
    return x * _sigmoid(x)


def _softplus(x):
    return jnp.maximum(x, 0.0) + jnp.log1p(jnp.exp(-jnp.abs(x)))


def _gelu_tanh(x):
    c = 0.7978845608028654
    return x * (0.5 * (1.0 + jnp.tanh(c * (x + 0.044715 * (x * x * x)))))


def _mod_row(tile, tile_rows):
    r0 = tile * tile_rows
    return jnp.where(r0 < N_PROMPT, 0, 1 + (r0 - N_PROMPT) // DEC_SEQ)


def _mod_spec(layer, which, tile_rows, tile0=0, width=D_MODEL):
    base = (layer * 6 + which) * MOD_ROWS
    if width == D_MODEL:
        return pl.BlockSpec((1, 1, width), lambda *ids: (base + _mod_row(tile0 + ids[0], tile_rows), 0, 0))
    return pl.BlockSpec((1, 1, width), lambda i, j: (base + _mod_row(tile0 + i, tile_rows), 0, j))


def _row_spec():
    return pl.BlockSpec((1, D_MODEL), lambda *ids: (0, 0))


def _norm_mod(x, g, sc, sh):
    ms = jnp.mean(x * x, axis=-1, keepdims=True)
    y = (x * lax.rsqrt(ms + EPS)) * g
    return (y * (1.0 + sc) + sh).astype(BF16)


def _params(*sem):
    return pltpu.CompilerParams(dimension_semantics=sem, vmem_limit_bytes=VMEM_LIMIT)


def _stack_kernel(p_ref, s_ref, g_ref, sc_ref, sh_ref, o_ref, h_ref, *, prompt_tiles):
    def emit(x):
        o_ref[...] = x
        h_ref[...] = _norm_mod(x, g_ref[...], sc_ref[0], sh_ref[0])

    @pl.when(pl.program_id(0) < prompt_tiles)
    def _():
        emit(p_ref[...])

    @pl.when(pl.program_id(0) >= prompt_tiles)
    def _():
        emit(s_ref[...])


def _stack_tokens(xp, xs, g, mod):
    tm = 512
    pt = N_PROMPT // tm
    return pl.pallas_call(
        functools.partial(_stack_kernel, prompt_tiles=pt), grid=(TOKENS // tm,),
        in_specs=[pl.BlockSpec((tm, D_MODEL), lambda i: (jnp.minimum(i, pt - 1), 0)),
                  pl.BlockSpec((tm, D_MODEL), lambda i: (jnp.maximum(i - pt, 0), 0)),
                  _row_spec(), _mod_spec(0, 1, tm), _mod_spec(0, 0, tm)],
        out_specs=[pl.BlockSpec((tm, D_MODEL), lambda i: (i, 0)),
                   pl.BlockSpec((tm, D_MODEL), lambda i: (i, 0))],
        out_shape=[jax.ShapeDtypeStruct((TOKENS, D_MODEL), F32),
                   jax.ShapeDtypeStruct((TOKENS, D_MODEL), BF16)],
        compiler_params=_params("arbitrary"),
        name="stack_tokens",
    )(xp, xs, g, mod, mod)


def _mod_kernel(c_ref, w_ref, b_ref, o_ref):
    s = _silu(c_ref[...]).astype(BF16)
    o_ref[0, 0] = jnp.dot(s, w_ref[0].astype(BF16), preferred_element_type=F32) + b_ref[0]


def _modulation(c3, w_mod, b_mod):
    tn = 1024
    nb = D_MODEL // tn
    out = pl.pallas_call(
        _mod_kernel,
        grid=(DEPTH, 6, nb),
        in_specs=[pl.BlockSpec((MOD_ROWS, D_MODEL), lambda l, k, j: (0, 0)),
                  pl.BlockSpec((1, D_MODEL, tn), lambda l, k, j: (l, 0, k * nb + j)),
                  pl.BlockSpec((1, 1, tn), lambda l, k, j: (l, 0, k * nb + j))],
        out_specs=pl.BlockSpec((1, 1, MOD_ROWS, tn), lambda l, k, j: (l, k, 0, j)),
        out_shape=jax.ShapeDtypeStruct((DEPTH, 6, MOD_ROWS, D_MODEL), F32),
        compiler_params=_params("arbitrary", "arbitrary", "arbitrary"),
        name="modulation",
    )(c3, w_mod, b_mod.reshape(DEPTH, 1, 6 * D_MODEL))
    return out.reshape(DEPTH * 6 * MOD_ROWS, 1, D_MODEL)


INPROJ_TN = 512


def _relayout_kernel(wt_ref, wlt_ref, o_ref, ol_ref):
    o_ref[...] = wt_ref[...].T.astype(BF16)

    @pl.when(pl.program_id(0) == 0)
    def _():
        ol_ref[...] = wlt_ref[...].T.astype(BF16)


def _feature_major_to_bf16(wt, widx, n_main, n_low):
    tn = INPROJ_TN
    return pl.pallas_call(
        _relayout_kernel, grid=(n_main // tn,),
        in_specs=[pl.BlockSpec((None, tn, D_MODEL), lambda j: (widx, j, 0)),
                  pl.BlockSpec((None, n_low, D_MODEL), lambda j: (widx, n_main // n_low, 0))],
        out_specs=[pl.BlockSpec((None, D_MODEL, tn), lambda j: (0, 0, j)),
                   pl.BlockSpec((D_MODEL, n_low), lambda j: (0, 0))],
        out_shape=[jax.ShapeDtypeStruct((1, D_MODEL, n_main), BF16),
                   jax.ShapeDtypeStruct((D_MODEL, n_low), BF16)],
        compiler_params=_params("arbitrary"),
        name="weight_relayout",
    )(wt, wt)


def _inproj_kernel(h_ref, w_ref, o_ref):
    o_ref[...] = jnp.dot(h_ref[...], w_ref[...].astype(BF16), preferred_element_type=F32).astype(o_ref.dtype)


def _inproj_lowrank_kernel(h_ref, w_ref, wl_ref, o_ref, ol_ref):
    @pl.when(pl.program_id(1) == 0)
    def _():
        ol_ref[...] = jnp.dot(h_ref[...], wl_ref[...].astype(BF16), preferred_element_type=F32)

    o_ref[...] = jnp.dot(h_ref[...], w_ref[...].astype(BF16), preferred_element_type=F32).astype(o_ref.dtype)


def _inproj(h, w, widx, groups, out_dtype, w_low=None):
    tm, tn = 2048, INPROJ_TN
    (b0, n0), (b1, _) = (groups + [(0, 0)])[:2]
    n_blocks = sum(n for _, n in groups)

    def wblock(j):
        return jnp.where(j < n0, b0 + j, b1 + j - n0) if len(groups) > 1 else b0 + j

    in_specs = [pl.BlockSpec((tm, D_MODEL), lambda i, j: (i, 0)),
                pl.BlockSpec((None, D_MODEL, tn), lambda i, j: (widx, 0, wblock(j)))]
    out_specs = pl.BlockSpec((tm, tn), lambda i, j: (i, j))
    out_shape = jax.ShapeDtypeStruct((TOKENS, n_blocks * tn), out_dtype)
    args = [h, w]
    body = _inproj_kernel
    if w_low is not None:
        nl = w_low.shape[1]
        in_specs.append(pl.BlockSpec((D_MODEL, nl), lambda i, j: (0, 0)))
        out_specs = [out_specs, pl.BlockSpec((tm, nl), lambda i, j: (i, 0))]
        out_shape = [out_shape, jax.ShapeDtypeStruct((TOKENS, nl), F32)]
        args.append(w_low)
        body = _inproj_lowrank_kernel
    return pl.pallas_call(
        body, grid=(TOKENS // tm, n_blocks), in_specs=in_specs, out_specs=out_specs, out_shape=out_shape,
        compiler_params=_params("parallel", "arbitrary"),
        name="inproj",
    )(*args)


def _outproj_kernel(a_ref, w_ref, x_ref, gate_ref, g_ref, sc_ref, sh_ref, o_ref, h_ref, w_scr):
    @pl.when(pl.program_id(0) == 0)
    def _():
        w_scr[...] = w_ref[...].astype(BF16)

    acc = jnp.dot(a_ref[...], w_scr[...], preferred_element_type=F32)
    y = x_ref[...] + gate_ref[0] * acc
    o_ref[...] = y
    h_ref[...] = _norm_mod(y, g_ref[...], sc_ref[0], sh_ref[0])


def _outproj(a, w, widx, x, mod, layer, g2):
    tm = 512
    k = a.shape[1]
    return pl.pallas_call(
        _outproj_kernel, grid=(TOKENS // tm,),
        in_specs=[pl.BlockSpec((tm, k), lambda i: (i, 0)),
                  pl.BlockSpec((None, k, D_MODEL), lambda i: (widx, 0, 0), pipeline_mode=pl.Buffered(1)),
                  pl.BlockSpec((tm, D_MODEL), lambda i: (i, 0)),
                  _mod_spec(layer, 2, tm), _row_spec(), _mod_spec(layer, 4, tm), _mod_spec(layer, 3, tm)],
        out_specs=[pl.BlockSpec((tm, D_MODEL), lambda i: (i, 0)),
                   pl.BlockSpec((tm, D_MODEL), lambda i: (i, 0))],
        out_shape=[jax.ShapeDtypeStruct((TOKENS, D_MODEL), F32),
                   jax.ShapeDtypeStruct((TOKENS, D_MODEL), BF16)],
        scratch_shapes=[pltpu.VMEM((k, D_MODEL), BF16)],
        input_output_aliases={2: 0},
        compiler_params=_params("arbitrary"),
        name="outproj",
    )(a, w, x, mod, g2, mod, mod)


def _ffn_up_kernel(h_ref, wg_ref, wu_ref, o_ref):
    h = h_ref[...]
    gt = jnp.dot(h, wg_ref[...].astype(BF16), preferred_element_type=F32)
    up = jnp.dot(h, wu_ref[...].astype(BF16), preferred_element_type=F32)
    o_ref[...] = (_silu(gt) * up).astype(BF16)


def _ffn_up(h, layer, w_in):
    tm, tf = 2048, 512
    nf = D_FF // tf
    return pl.pallas_call(
        _ffn_up_kernel, grid=(TOKENS // tm, nf),
        in_specs=[pl.BlockSpec((tm, D_MODEL), lambda i, j: (i, 0)),
                  pl.BlockSpec((None, D_MODEL, tf), lambda i, j: (layer, 0, j)),
                  pl.BlockSpec((None, D_MODEL, tf), lambda i, j: (layer, 0, nf + j))],
        out_specs=pl.BlockSpec((tm, tf), lambda i, j: (i, j)),
        out_shape=jax.ShapeDtypeStruct((TOKENS, D_FF), BF16),
        compiler_params=_params("parallel", "arbitrary"),
        name="ffn_up",
    )(h, w_in, w_in)


def _ffn_down_kernel(a_ref, w_ref, x_ref, gate_ref, o_ref, w_scr):
    @pl.when(pl.program_id(1) == 0)
    def _():
        w_scr[...] = w_ref[...].astype(BF16)

    acc = jnp.dot(a_ref[...], w_scr[...], preferred_element_type=F32)
    o_ref[...] = x_ref[...] + gate_ref[0] * acc


def _ffn_down(a, w, x, mod, layer):
    tm, tn = 512, 512
    base = (layer * 6 + 5) * MOD_ROWS
    return pl.pallas_call(
        _ffn_down_kernel, grid=(D_MODEL // tn, TOKENS // tm),
        in_specs=[pl.BlockSpec((tm, D_FF), lambda j, i: (i, 0)),
                  pl.BlockSpec((None, D_FF, tn), lambda j, i: (layer, 0, j)),
                  pl.BlockSpec((tm, tn), lambda j, i: (i, j)),
                  pl.BlockSpec((1, 1, tn), lambda j, i: (base + _mod_row(i, tm), 0, j))],
        out_specs=pl.BlockSpec((tm, tn), lambda j, i: (i, j)),
        out_shape=jax.ShapeDtypeStruct((TOKENS, D_MODEL), F32),
        scratch_shapes=[pltpu.VMEM((D_FF, tn), BF16)],
        input_output_aliases={2: 0},
        compiler_params=_params("arbitrary", "arbitrary"),
        name="ffn_down",
    )(a, w, x, mod)


def _norm_mod_kernel(x_ref, g_ref, sc_ref, sh_ref, h_ref):
    h_ref[...] = _norm_mod(x_ref[...], g_ref[...], sc_ref[0], sh_ref[0])


def _norm_mod_rows(x, g, mod, layer):
    tm = 512
    return pl.pallas_call(
        _norm_mod_kernel, grid=(TOKENS // tm,),
        in_specs=[pl.BlockSpec((tm, D_MODEL), lambda i: (i, 0)),
                  _row_spec(), _mod_spec(layer, 1, tm), _mod_spec(layer, 0, tm)],
        out_specs=pl.BlockSpec((tm, D_MODEL), lambda i: (i, 0)),
        out_shape=jax.ShapeDtypeStruct((TOKENS, D_MODEL), BF16),
        compiler_params=_params("parallel"),
        name="norm_mod",
    )(x, g, mod, mod)


def _final_norm_kernel(x_ref, g_ref, o_ref):
    x = x_ref[...]
    ms = jnp.mean(x * x, axis=-1, keepdims=True)
    o_ref[...] = (x * lax.rsqrt(ms + EPS)) * g_ref[...]


def _final_norm(x, g, row0, rows):
    tm = 512
    t0 = row0 // tm
    return pl.pallas_call(
        _final_norm_kernel, grid=(rows // tm,),
        in_specs=[pl.BlockSpec((tm, D_MODEL), lambda i: (t0 + i, 0)), _row_spec()],
        out_specs=pl.BlockSpec((tm, D_MODEL), lambda i: (i, 0)),
        out_shape=jax.ShapeDtypeStruct((rows, D_MODEL), F32),
        compiler_params=_params("parallel"),
        name="final_norm",
    )(x, g)


def _chunk_cumsum(g, sub, reverse):
    rows, lanes = g.shape
    tiles = rows // SUBLANES
    x = g.reshape(tiles, SUBLANES, lanes)
    s = 1
    while s < SUBLANES:
        if reverse:
            x = x + jnp.where(sub < SUBLANES - s, pltpu.roll(x, SUBLANES - s, 1), 0.0)
        else:
            x = x + jnp.where(sub >= s, pltpu.roll(x, s, 1), 0.0)
        s *= 2
    parts = [None] * tiles
    carry = None
    for j in (range(tiles - 1, -1, -1) if reverse else range(tiles)):
        xj = x[j] if carry is None else x[j] + carry
        parts[j] = xj
        carry = xj[0:1, :] if reverse else xj[SUBLANES - 1:SUBLANES, :]
    return jnp.concatenate(parts, axis=0)


def _gla_kernel(*refs, variant, layer, n_seq, seq_len, hb, dv, has_s0):
    refs = list(refs)
    if variant == "hgrn":
        q_ref, ff_ref, fb_ref, v_ref, gate_ref, lb_ref, gain_ref = refs[:7]
        refs = refs[7:]
        gate_in = (ff_ref, fb_ref)
    else:
        q_ref, k_ref, v_ref, gate_ref, lr_ref, wa_ref, ba_ref, gain_ref = refs[:8]
        refs = refs[8:]
    s0_ref = refs.pop(0) if has_s0 else None
    refs.pop(0)
    out_ref, st_ref = refs[:2]
    o_scr, s_scr = refs[2:4]
    g_scr = refs[4:]

    t = seq_len
    n = t // SCAN_CHUNK
    half = SCAN_CHUNK // 2
    sub = lax.broadcasted_iota(jnp.int32, (SCAN_CHUNK // SUBLANES, SUBLANES, A_DK), 1)
    ri = lax.broadcasted_iota(jnp.int32, (SCAN_CHUNK, SCAN_CHUNK), 0)
    ci = lax.broadcasted_iota(jnp.int32, (SCAN_CHUNK, SCAN_CHUNK), 1)
    masks = (ci <= ri, ci >= ri)
    nt_dims = (((1,), (1,)), ((), ()))
    tn_dims = (((0,), (0,)), ((), ()))
    gain = gain_ref[...]

    rows_total = n_seq * t
    if variant == "hgrn":
        lbr = lb_ref[...]
        ex = jnp.exp(lbr - jnp.max(lbr, axis=0, keepdims=True))
        lb = jnp.sum(ex[:layer + 1], axis=0, keepdims=True) / jnp.sum(ex, axis=0, keepdims=True)
    else:
        def decay_tile(i, carry):
            rows = pl.ds(pl.multiple_of(i * TIME_TILE, TIME_TILE), TIME_TILE)
            lr = lr_ref[rows, :].astype(BF16)
            for d in range(2):
                z = jnp.dot(lr, wa_ref[d].astype(BF16), preferred_element_type=F32) + ba_ref[d:d + 1, :]
                g_scr[d][rows, :] = -_softplus(-z) * (LOG2_E / GLA_GATE_NORM)
            return carry

        lax.fori_loop(0, rows_total // TIME_TILE, decay_tile, 0)

    def chunk_step(c, s, h, d):
        rows = pl.ds(pl.multiple_of(s * t + c * SCAN_CHUNK, SCAN_CHUNK), SCAN_CHUNK)
        kc = slice(h * A_DK, (h + 1) * A_DK)
        vc = slice(h * dv, (h + 1) * dv)
        chain = (s * hb + h) * 2 + d
        q = q_ref[rows, kc] * (A_DK ** -0.5)
        if variant == "hgrn":
            lbh = lb[:, kc]
            f = lbh + (1.0 - lbh) * _sigmoid(gate_in[d][rows, kc])
            k = 1.0 - f
            g = jnp.log(f) * LOG2_E
        else:
            k = k_ref[rows, kc]
            g = g_scr[d][rows, kc]
        b = _chunk_cumsum(g, sub, d == 1)
        tot = b[0:1, :] if d == 1 else b[SCAN_CHUNK - 1:SCAN_CHUNK, :]
        mid = b[half:half + 1, :] if d == 1 else b[half - 1:half, :]
        qs = (q * jnp.exp2(b)).astype(BF16)
        qa = (q * jnp.exp2(b - mid)).astype(BF16)
        ka = (k * jnp.exp2(mid - b)).astype(BF16)
        kt = (k * jnp.exp2(tot - b)).astype(BF16)
        vb = v_ref[rows, vc].astype(BF16)
        s_t = s_scr[chain]
        att = lax.dot_general(qa, ka, nt_dims, preferred_element_type=F32)
        att = jnp.where(masks[d], att, 0.0).astype(BF16)
        o = jnp.dot(jnp.concatenate([qs, att], axis=1),
                    jnp.concatenate([s_t.astype(BF16), vb], axis=0), preferred_element_type=F32)
        decay = jnp.broadcast_to(jnp.exp2(tot), (A_DK, A_DK)).T
        if dv != A_DK:
            decay = jnp.concatenate([decay] * (dv // A_DK), axis=1)
        s_new = decay * s_t + lax.dot_general(kt, vb, tn_dims, preferred_element_type=F32)
        return chain, rows, vc, o, s_new

    def finish(rows, vc, o):
        o = o + o_scr[rows, vc]
        ms = jnp.mean(o * o, axis=-1, keepdims=True)
        o = (o * lax.rsqrt(ms + EPS)) * gain
        return (o * _silu(gate_ref[rows, vc].astype(F32))).astype(out_ref.dtype)

    chains = [(s, h, d) for d in range(2) for s in range(n_seq) for h in range(hb)]

    def first_half(i, carry):
        done = [chunk_step(i if d == 0 else n - 1 - i, s, h, d) for s, h, d in chains]
        for chain, rows, vc, o, s_new in done:
            s_scr[chain] = s_new
            o_scr[rows, vc] = o + 0.0
        return carry

    def second_half(i, carry):
        done = [chunk_step(i if d == 0 else n - 1 - i, s, h, d) for s, h, d in chains]
        outs = [finish(rows, vc, o) for _, rows, vc, o, _ in done]
        for (chain, rows, vc, _, s_new), out in zip(done, outs):
            s_scr[chain] = s_new
            out_ref[rows, vc] = out
        return carry

    for s in range(n_seq):
        for h in range(hb):
            for d in range(2):
                chain = (s * hb + h) * 2 + d
                s_scr[chain] = s0_ref[s, 0, d, h] if has_s0 else jnp.zeros((A_DK, dv), F32)
    lax.fori_loop(0, n // 2, first_half, 0)
    lax.fori_loop(n // 2, n, second_half, 0)
    for s in range(n_seq):
        for h in range(hb):
            for d in range(2):
                st_ref[s, 0, d, h] = s_scr[(s * hb + h) * 2 + d]


def _gla_scan(variant, layer, e, proj, proj16, lowrank, prev, s0, n_seq, seq_len, row0, small, seq_blk, hb):
    t = seq_len
    rows = seq_blk * t
    tb0 = row0 // rows
    heads, dv = (A_HEADS, A_DV) if variant == "hgrn" else (B_HEADS, B_DV)

    def col_spec(width, col0):
        blk0 = col0 // (hb * width)
        return pl.BlockSpec((rows, hb * width), lambda s, h: (tb0 + s, blk0 + h))

    if variant == "hgrn":
        o_q, o_ff, o_fb = 0, A_KDIM, 2 * A_KDIM
        o_v, o_g = 0, A_VDIM
        in_specs = [col_spec(A_DK, o_q), col_spec(A_DK, o_ff), col_spec(A_DK, o_fb),
                    col_spec(dv, o_v), col_spec(dv, o_g),
                    pl.BlockSpec((DEPTH + 1, hb * A_DK), lambda s, h: (0, h)),
                    pl.BlockSpec((1, dv), lambda s, h: (0, 0))]
        args = [proj] * 3 + [proj16] * 2 + [small["lb"], small["gain"]]
        out_col0 = 0
        decay_scratch = []
    else:
        o_q, o_k = 3 * A_KDIM, 3 * A_KDIM + B_KDIM
        o_v, o_g = 2 * A_VDIM, 2 * A_VDIM + B_VDIM
        in_specs = [col_spec(B_DK, o_q), col_spec(B_DK, o_k),
                    col_spec(dv, o_v), col_spec(dv, o_g),
                    pl.BlockSpec((rows, 2 * GLA_RANK), lambda s, h: (tb0 + s, 0)),
                    pl.BlockSpec((2, 2 * GLA_RANK, hb * B_DK), lambda s, h: (0, 0, h)),
                    pl.BlockSpec((2, hb * B_DK), lambda s, h: (0, h)),
                    pl.BlockSpec((1, dv), lambda s, h: (0, 0))]
        args = [proj] * 2 + [proj16] * 2 + [lowrank, small["wa"], small["ba"], small["gain"]]
        out_col0 = A_VDIM
        decay_scratch = [pltpu.VMEM((rows, hb * B_DK), F32)] * 2
    has_s0 = s0 is not None
    if has_s0:
        in_specs.append(pl.BlockSpec((seq_blk, 1, 2, hb, A_DK, dv), lambda s, h: (s, e, 0, h, 0, 0)))
        args.append(s0)
    in_specs.append(pl.BlockSpec(memory_space=pl.ANY))
    aliases = {len(args): 0}
    args.append(prev)
    out_blk0 = out_col0 // (hb * dv)
    merged, states = pl.pallas_call(
        functools.partial(_gla_kernel, variant=variant, layer=layer, n_seq=seq_blk, seq_len=t, hb=hb, dv=dv,
                          has_s0=has_s0),
        grid=(n_seq // seq_blk, heads // hb),
        in_specs=in_specs,
        out_specs=[pl.BlockSpec((rows, hb * dv), lambda s, h: (tb0 + s, out_blk0 + h)),
                   pl.BlockSpec((seq_blk, 1, 2, hb, A_DK, dv), lambda s, h: (s, 0, 0, h, 0, 0))],
        out_shape=[jax.ShapeDtypeStruct((TOKENS, A_VDIM + B_VDIM), BF16),
                   jax.ShapeDtypeStruct((n_seq, 1, 2, heads, A_DK, dv), F32)],
        scratch_shapes=[pltpu.VMEM((rows, hb * dv), F32), pltpu.VMEM((seq_blk * hb * 2, A_DK, dv), F32)]
                       + decay_scratch,
        input_output_aliases=aliases,
        compiler_params=_params("parallel", "arbitrary"),
        name=f"{variant}_scan",
    )(*args)
    return merged, states


def _affine_scan_in_tiles(a, u, sub, reverse):
    s = 1
    while s < SUBLANES:
        if reverse:
            ok, shift = sub < SUBLANES - s, SUBLANES - s
        else:
            ok, shift = sub >= s, s
        u = u + a * jnp.where(ok, pltpu.roll(u, shift, 1), 0.0)
        a = a * jnp.where(ok, pltpu.roll(a, shift, 1), 1.0)
        s *= 2
    return a, u


def _lru_kernel(*refs, n_seq, seq_len, row_len, has_s0):
    refs = list(refs)
    xr_ref, gate_ref, cw_ref, cb_ref, rgw_ref, rgb_ref, igw_ref, igb_ref, lam_ref = refs[:9]
    refs = refs[9:]
    s0_ref = refs.pop(0) if has_s0 else None
    refs.pop(0)
    out_ref, st_ref = refs[:2]
    xc_scr, y_scr, in_scr, out_scr = refs[2:]

    w = LRU_BW
    tl = TIME_TILE
    tiles_per_seq = seq_len // tl
    n_tiles = n_seq * tiles_per_seq
    n_groups = tl // SUBLANES
    lane_tiles = w // LANES
    groups_per_row = row_len // SUBLANES
    sub2 = lax.broadcasted_iota(jnp.int32, (n_groups // SUBLANES, SUBLANES, w), 1)
    gpos = lax.broadcasted_iota(jnp.int32, (n_groups, w), 0)
    row_first = gpos % groups_per_row == 0
    row_last = gpos % groups_per_row == groups_per_row - 1
    cw = cw_ref[...]

    def slabs_of(x):
        for lt in range(lane_tiles):
            in_scr[lt] = x[:, lt * LANES:(lt + 1) * LANES]
        return [jnp.concatenate([in_scr[lt, pl.ds(r, n_groups, stride=SUBLANES), :]
                                 for lt in range(lane_tiles)], axis=1) for r in range(SUBLANES)]

    def time_order(slabs):
        for r in range(SUBLANES):
            for lt in range(lane_tiles):
                out_scr[lt, pl.ds(r, n_groups, stride=SUBLANES), :] = slabs[r][:, lt * LANES:(lt + 1) * LANES]
        return jnp.concatenate([out_scr[lt] for lt in range(lane_tiles)], axis=1)

    def split(x):
        return [x[r * n_groups:(r + 1) * n_groups, :] for r in range(SUBLANES)]

    def conv(x):
        back = lambda z: jnp.where(row_first, 0.0, pltpu.roll(z, 1, 0))
        fwd = lambda z: jnp.where(row_last, 0.0, pltpu.roll(z, n_groups - 1, 0))
        ext = [back(x[SUBLANES - 2]), back(x[SUBLANES - 1])] + x + [fwd(x[0])]
        out = []
        for r in range(SUBLANES):
            xc = cb_ref[...] + ext[r] * cw[0:1, :]
            xc = xc + ext[r + 1] * cw[1:2, :]
            xc = xc + ext[r + 2] * cw[2:3, :]
            out.append(xc + ext[r + 3] * cw[3:4, :])
        return jnp.concatenate(out, axis=0)

    def tile_pass(d):
        nsp = -LRU_C * _softplus(-lam_ref[d:d + 1, :])
        edge = SUBLANES - 1 if d == 0 else 0
        order = range(SUBLANES) if d == 0 else range(SUBLANES - 1, -1, -1)

        def tile_body(i, hprev):
            tile = i if d == 0 else n_tiles - 1 - i
            seq = tile // tiles_per_seq
            starts = (tile % tiles_per_seq) == (0 if d == 0 else tiles_per_seq - 1)
            h0 = s0_ref[pl.ds(seq, 1), 0, d:d + 1, :].reshape(1, w) if has_s0 else jnp.zeros((1, w), F32)
            hprev = jnp.where(starts, h0, hprev)
            r0 = pl.multiple_of(tile * tl, tl)
            rows = pl.ds(r0, tl)
            if d == 0:
                xc = conv(slabs_of(xr_ref[rows, :]))
                xc_scr[rows, :] = xc
            else:
                xc = xc_scr[rows, :]
            xcb = xc.astype(BF16)
            r = _sigmoid(jnp.dot(xcb, rgw_ref[d, 0].astype(BF16), preferred_element_type=F32)
                         + rgb_ref[d:d + 1, :])
            ig = _sigmoid(jnp.dot(xcb, igw_ref[d, 0].astype(BF16), preferred_element_type=F32)
                          + igb_ref[d:d + 1, :])
            log_a = r * nsp
            a = jnp.exp(log_a)
            u = jnp.sqrt(-jnp.tanh(log_a) * (a * a + 1.0)) * (ig * xc)
            a_s, u_s = split(a), split(u)
            acc_a, acc_u = [None] * SUBLANES, [None] * SUBLANES
            pa_ = pu_ = None
            for r in order:
                if pa_ is None:
                    acc_a[r], acc_u[r] = a_s[r], u_s[r]
                else:
                    acc_u[r] = a_s[r] * pu_ + u_s[r]
                    acc_a[r] = a_s[r] * pa_
                pa_, pu_ = acc_a[r], acc_u[r]
            n2 = n_groups // SUBLANES
            a2, u2 = _affine_scan_in_tiles(acc_a[edge].reshape(n2, SUBLANES, w),
                                           acc_u[edge].reshape(n2, SUBLANES, w), sub2, d == 1)
            pa, pu = [None] * n2, [None] * n2
            ca = cu = None
            for j in (range(n2 - 1, -1, -1) if d == 1 else range(n2)):
                aj, uj = a2[j], u2[j]
                if ca is not None:
                    uj = uj + aj * cu
                    aj = aj * ca
                pa[j], pu[j] = aj, uj
                ca, cu = aj[edge:edge + 1, :], uj[edge:edge + 1, :]
            c = jnp.concatenate(pu, axis=0) + jnp.concatenate(pa, axis=0) * hprev
            if d == 0:
                cin = jnp.where(gpos >= 1, pltpu.roll(c, 1, 0), hprev)
            else:
                cin = jnp.where(gpos < n_groups - 1, pltpu.roll(c, n_groups - 1, 0), hprev)
            hs = jnp.concatenate([acc_u[r] + acc_a[r] * cin for r in range(SUBLANES)], axis=0)
            if d == 0:
                y_scr[rows, :] = hs
            else:
                gate = jnp.concatenate(slabs_of(gate_ref[rows, :]), axis=0)
                y = (y_scr[rows, :] + hs) * _gelu_tanh(gate)
                out_ref[rows, :] = time_order(split(y)).astype(out_ref.dtype)
            hlast = c[n_groups - 1:n_groups, :] if d == 0 else c[0:1, :]
            st_ref[pl.ds(seq, 1), 0, d:d + 1, :] = hlast.reshape(1, 1, w)
            return hlast

        lax.fori_loop(0, n_tiles, tile_body, jnp.zeros((1, w), F32))

    tile_pass(0)
    tile_pass(1)


def _lru_scan(e, proj, gate_proj, prev, s0, n_seq, seq_len, row_len, row0, p, seq_blk):
    t = seq_blk * seq_len
    tb0 = row0 // t
    has_s0 = s0 is not None
    in_specs = [pl.BlockSpec((t, LRU_BW), lambda s, n: (tb0 + s, n)),
                pl.BlockSpec((t, LRU_BW), lambda s, n: (tb0 + s, n)),
                pl.BlockSpec((CONV_W, LRU_BW), lambda s, n: (0, n)),
                pl.BlockSpec((1, LRU_BW), lambda s, n: (0, n)),
                pl.BlockSpec((2, 1, LRU_BW, LRU_BW), lambda s, n: (0, n, 0, 0)),
                pl.BlockSpec((2, LRU_BW), lambda s, n: (0, n)),
                pl.BlockSpec((2, 1, LRU_BW, LRU_BW), lambda s, n: (0, n, 0, 0)),
                pl.BlockSpec((2, LRU_BW), lambda s, n: (0, n)),
                pl.BlockSpec((2, LRU_BW), lambda s, n: (0, n))]
    args = [proj, gate_proj, p["conv_w"], p["conv_b"], p["rg_w"], p["rg_b"], p["ig_w"], p["ig_b"], p["lam"]]
    if has_s0:
        in_specs.append(pl.BlockSpec((seq_blk, 1, 2, LRU_BW), lambda s, n: (s, e, 0, n)))
        args.append(s0)
    in_specs.append(pl.BlockSpec(memory_space=pl.ANY))
    aliases = {len(args): 0}
    args.append(prev)
    return pl.pallas_call(
        functools.partial(_lru_kernel, n_seq=seq_blk, seq_len=seq_len, row_len=row_len, has_s0=has_s0),
        grid=(n_seq // seq_blk, LRU_BLOCKS),
        in_specs=in_specs,
        out_specs=[pl.BlockSpec((t, LRU_BW), lambda s, n: (tb0 + s, n)),
                   pl.BlockSpec((seq_blk, 1, 2, LRU_BW), lambda s, n: (s, 0, 0, n))],
        out_shape=[jax.ShapeDtypeStruct((TOKENS, LRU_WIDTH), BF16),
                   jax.ShapeDtypeStruct((n_seq, 1, 2, LRU_WIDTH), F32)],
        scratch_shapes=[pltpu.VMEM((t, LRU_BW), F32), pltpu.VMEM((t, LRU_BW), F32),
                        pltpu.VMEM((LRU_BW // LANES, TIME_TILE, LANES), F32),
                        pltpu.VMEM((LRU_BW // LANES, TIME_TILE, LANES), F32)],
        input_output_aliases=aliases,
        compiler_params=_params("parallel", "arbitrary"),
        name="lru_scan",
    )(*args)


def kernel(x_prompt, x_sample, state_hgrn, state_gla, state_rglru, c, c_ctx, norm1_g, norm2_g, w_mod, b_mod,
           ffn_w_in, ffn_w_out, hgrn_lower_bounds, even_w_in, gla_w_alpha, gla_b_alpha, hgrn_norm_g,
           gla_norm_g, even_w_out, odd_w_in, conv_w, conv_b, rg_w, rg_b, ig_w, ig_b, lru_lambda, odd_w_out,
           final_norm_g):
    c3 = jnp.concatenate([c_ctx[None, :], c, jnp.zeros((MOD_ROWS - 1 - DEC_BATCH, D_MODEL), F32)], axis=0)
    mod = _modulation(c3, w_mod, b_mod)
    x, h = _stack_tokens(x_prompt.reshape(N_PROMPT, D_MODEL), x_sample.reshape(N_SAMPLE, D_MODEL),
                         norm1_g[0][None, :], mod)

    new_hgrn, new_gla, new_lru = [], [], []
    y_prompt = y_sample = None
    for l in range(DEPTH):
        e = l // 2
        if l % 2 == 0:
            w16, w16_low = _feature_major_to_bf16(jnp.swapaxes(even_w_in, 1, 2), e, EVEN_MAIN, 2 * GLA_RANK)
            nb = A_KDIM // INPROJ_TN
            proj, lowrank = _inproj(h, w16, 0, [(0, 3 * nb), (5 * nb, nb)], F32, w16_low)
            proj16 = _inproj(h, w16, 0, [(3 * nb, 2 * nb), (6 * nb, 2 * nb)], BF16)
            wa = jnp.zeros((2, 2 * GLA_RANK, B_KDIM), F32)
            wa = wa.at[0, :GLA_RANK].set(gla_w_alpha[e, 0]).at[1, GLA_RANK:].set(gla_w_alpha[e, 1])
            small_a = {"lb": hgrn_lower_bounds, "gain": hgrn_norm_g[e][None, :]}
            small_b = {"wa": wa, "ba": gla_b_alpha[e], "gain": gla_norm_g[e][None, :]}
            merged = jnp.zeros((TOKENS, A_VDIM + B_VDIM), BF16)
            merged, sa = _gla_scan("hgrn", l, e, proj, proj16, None, merged, None, BATCH, SEQ, 0, small_a, 8, 1)
            merged, sb = _gla_scan("gla", l, e, proj, proj16, lowrank, merged, None, BATCH, SEQ, 0, small_b, 4, 1)
            merged, _ = _gla_scan("hgrn", l, e, proj, proj16, None, merged, state_hgrn, DEC_BATCH, DEC_SEQ, N_PROMPT,
                                  small_a, 2, 2)
            merged, _ = _gla_scan("gla", l, e, proj, proj16, lowrank, merged, state_gla, DEC_BATCH, DEC_SEQ, N_PROMPT,
                                  small_b, 2, 1)
            new_hgrn.append(sa)
            new_gla.append(sb)
            w_out = even_w_out
        else:
            nb = LRU_WIDTH // INPROJ_TN
            proj = _inproj(h, odd_w_in, e, [(0, nb)], F32)
            proj16 = _inproj(h, odd_w_in, e, [(nb, nb)], F32)
            p = {"conv_w": conv_w[e], "conv_b": conv_b[e][None, :], "rg_w": rg_w[e], "rg_b": rg_b[e],
                 "ig_w": ig_w[e], "ig_b": ig_b[e], "lam": lru_lambda[e]}
            merged = jnp.zeros((TOKENS, LRU_WIDTH), BF16)
            merged, sr = _lru_scan(e, proj, proj16, merged, None, BATCH, SEQ, SEQ, 0, p, 4)
            merged, _ = _lru_scan(e, proj, proj16, merged, state_rglru, DEC_BATCH, DEC_SEQ, GRID_W, N_PROMPT, p, 1)
            new_lru.append(sr)
            w_out = odd_w_out
        x, h = _outproj(merged, w_out, e, x, mod, l, norm2_g[l][None, :])
        hidden = _ffn_up(h, l, ffn_w_in)
        x = _ffn_down(hidden, ffn_w_out, x, mod, l)
        if l + 1 < DEPTH:
            h = _norm_mod_rows(x, norm1_g[l + 1][None, :], mod, l + 1)
    gf = final_norm_g[None, :]
    y_prompt = _final_norm(x, gf, 0, N_PROMPT)
    y_sample = _final_norm(x, gf, N_PROMPT, N_SAMPLE)
    dt = x_prompt.dtype
    return (y_prompt.reshape(BATCH, SEQ, D_MODEL), y_sample.reshape(DEC_BATCH, DEC_SEQ, D_MODEL),
            jnp.concatenate(new_hgrn, axis=1).astype(dt), jnp.concatenate(new_gla, axis=1).astype(dt),
            jnp.concatenate(new_lru, axis=1).astype(dt))
```
